```python
import jax, jax.numpy as jnp
from jax import lax
import numpy as np

D_MODEL = 2048
BATCH = 4
SEQ = 4096
DEPTH = 2

N_META = 16
HEAD_DIM = 128
N_HEADS_FOX = D_MODEL // (2 * HEAD_DIM)
N_HEADS_SB = D_MODEL // (2 * HEAD_DIM)
D_FOX = N_HEADS_FOX * HEAD_DIM
D_SB = N_HEADS_SB * HEAD_DIM
D_MIX = D_FOX + D_SB
D_IN = 3 * D_FOX + 3 * D_SB + N_HEADS_FOX
D_FF = 256 * ((8 * D_MODEL // 3 + 255) // 256)
QB = 128
EPS = 1e-6
SPLITS = [D_FOX, 2 * D_FOX, 3 * D_FOX, 3 * D_FOX + D_SB, 3 * D_FOX + 2 * D_SB, 3 * D_FOX + 3 * D_SB]

kernel_name = "hymba_fox_stickbreaking_macaron"


def rms_norm(x, g):
    xf = x.astype(jnp.float32)
    y = xf * lax.rsqrt(jnp.mean(xf * xf, axis=-1, keepdims=True) + EPS)
    return (y * g.astype(jnp.float32)).astype(x.dtype)


def swiglu(x, w_gate, w_up, w_down):
    return (jax.nn.silu(x @ w_gate) * (x @ w_up)) @ w_down


def to_heads(t, n_heads, pad):
    B, L, _ = t.shape
    t = t.reshape(B, L, n_heads, HEAD_DIM).transpose(0, 2, 1, 3)
    return jnp.pad(t, ((0, 0), (0, 0), (pad, 0), (0, 0)))


def to_blocks(t):
    B, H, Lp = t.shape[:3]
    t = t.reshape((B, H, Lp // QB, QB) + t.shape[3:])
    return jnp.moveaxis(t, 2, 0)


def from_blocks(o):
    nb, B, H, qb, dh = o.shape
    return o.transpose(1, 2, 0, 3, 4).reshape(B, H, nb * qb, dh)


def forgetting_attention(q, k, v, log_f, valid):
    Lp, dh = q.shape[2], q.shape[3]
    c = jnp.cumsum(log_f, axis=-1)
    key_pos = jnp.arange(Lp)
    scale = dh ** -0.5
    starts = jnp.arange(Lp // QB) * QB

    def one_block(args):
        qb, cb, start = args
        q_pos = start + jnp.arange(QB)
        s = jnp.einsum('bhqd,bhkd->bhqk', qb, k).astype(jnp.float32) * scale
        s = s + cb[..., None] - c[:, :, None, :]
        diag = key_pos[None, :] == q_pos[:, None]
        allowed = (key_pos[None, :] <= q_pos[:, None]) & (valid[None, :] | diag)
        p = jax.nn.softmax(jnp.where(allowed, s, -jnp.inf), axis=-1)
        return jnp.einsum('bhqk,bhkd->bhqd', p.astype(v.dtype), v)

    return from_blocks(lax.map(one_block, (to_blocks(q), to_blocks(c), starts)))


def stick_breaking_attention(q, k, v, valid):
    Lp, dh = q.shape[2], q.shape[3]
    key_pos = jnp.arange(Lp)
    scale = dh ** -0.5
    starts = jnp.arange(Lp // QB) * QB

    def one_block(args):
        qb, start = args
        q_pos = start + jnp.arange(QB)
        z = jnp.einsum('bhqd,bhkd->bhqk', qb, k).astype(jnp.float32) * scale
        allowed = (key_pos[None, :] < q_pos[:, None]) & valid[None, :]
        log_1m = jnp.where(allowed, jax.nn.log_sigmoid(-z), 0.0)
        later = lax.cumsum(log_1m, axis=3, reverse=True) - log_1m
        a = jnp.where(allowed, jnp.exp(jax.nn.log_sigmoid(z) + later), 0.0)
        return jnp.einsum('bhqk,bhkd->bhqd', a.astype(v.dtype), v)

    return from_blocks(lax.map(one_block, (to_blocks(q), starts)))


def head_group_norm(o, g, pad):
    o = o[:, :, pad:]
    of = o.astype(jnp.float32)
    of = of * lax.rsqrt(jnp.mean(of * of, axis=-1, keepdims=True) + EPS)
    B, H, L, dh = o.shape
    of = of.transpose(0, 2, 1, 3).reshape(B, L, H * dh)
    return (of * g.astype(jnp.float32)).astype(o.dtype)


def hybrid_mixer(xn, w_in, b_forget, g_fox, g_sb, w_out):
    B, L, _ = xn.shape
    proj = xn @ w_in
    q_f, k_f, v_f, q_s, k_s, v_s, f_logit = jnp.split(proj, SPLITS, axis=-1)
    pad = (-L) % QB
    valid = jnp.arange(L + pad) >= pad
    log_f = jax.nn.log_sigmoid((f_logit + b_forget).astype(jnp.float32)).transpose(0, 2, 1)
    log_f = jnp.pad(log_f, ((0, 0), (0, 0), (pad, 0)))
    o_f = forgetting_attention(to_heads(q_f, N_HEADS_FOX, pad), to_heads(k_f, N_HEADS_FOX, pad),
                               to_heads(v_f, N_HEADS_FOX, pad), log_f, valid)
    o_s = stick_breaking_attention(to_heads(q_s, N_HEADS_SB, pad), to_heads(k_s, N_HEADS_SB, pad),
                                   to_heads(v_s, N_HEADS_SB, pad), valid)
    o = jnp.concatenate([head_group_norm(o_f, g_fox, pad), head_group_norm(o_s, g_sb, pad)], axis=-1)
    return o @ w_out


def setup_inputs(seed: int = 0) -> dict:
    key = jax.random.key(seed)
    ks = jax.random.split(key, 20)
    f32 = jnp.float32
    nrm = lambda k, shape, scale: jax.random.normal(k, shape, f32) * scale
    gain = lambda k, shape: 1.0 + 0.02 * jax.random.normal(k, shape, f32)
    return {
        "x": jax.random.normal(ks[0], (BATCH, SEQ, D_MODEL), f32),
        "meta_tokens": nrm(ks[1], (N_META, D_MODEL), 1.0),
        "ffn1_norm": gain(ks[2], (DEPTH, D_MODEL)),
        "ffn1_w_gate": nrm(ks[3], (DEPTH, D_MODEL, D_FF), D_MODEL ** -0.5),
        "ffn1_w_up": nrm(ks[4], (DEPTH, D_MODEL, D_FF), D_MODEL ** -0.5),
        "ffn1_w_down": nrm(ks[5], (DEPTH, D_FF, D_MODEL), D_FF ** -0.5),
        "mix_norm": gain(ks[6], (DEPTH, D_MODEL)),
        "w_in": nrm(ks[7], (DEPTH, D_MODEL, D_IN), D_MODEL ** -0.5),
        "b_forget": jax.random.uniform(ks[8], (DEPTH, N_HEADS_FOX), f32, 1.0, 5.0),
        "g_fox": gain(ks[9], (DEPTH, D_FOX)),
        "g_sb": gain(ks[10], (DEPTH, D_SB)),
        "w_out": nrm(ks[11], (DEPTH, D_MIX, D_MODEL), D_MIX ** -0.5),
        "ffn2_norm": gain(ks[12], (DEPTH, D_MODEL)),
        "ffn2_w_gate": nrm(ks[13], (DEPTH, D_MODEL, D_FF), D_MODEL ** -0.5),
        "ffn2_w_up": nrm(ks[14], (DEPTH, D_MODEL, D_FF), D_MODEL ** -0.5),
        "ffn2_w_down": nrm(ks[15], (DEPTH, D_FF, D_MODEL), D_FF ** -0.5),
        "final_norm": gain(ks[16], (D_MODEL,)),
    }


def reference(x, meta_tokens, ffn1_norm, ffn1_w_gate, ffn1_w_up, ffn1_w_down, mix_norm, w_in,
              b_forget, g_fox, g_sb, w_out, ffn2_norm, ffn2_w_gate, ffn2_w_up, ffn2_w_down,
              final_norm):
    B = x.shape[0]
    meta = jnp.broadcast_to(meta_tokens[None].astype(x.dtype), (B, N_META, x.shape[-1]))
    h = jnp.concatenate([meta, x], axis=1)
    for l in range(DEPTH):
        h = h + 0.5 * swiglu(rms_norm(h, ffn1_norm[l]), ffn1_w_gate[l], ffn1_w_up[l], ffn1_w_down[l])
        h = h + hybrid_mixer(rms_norm(h, mix_norm[l]), w_in[l], b_forget[l], g_fox[l], g_sb[l], w_out[l])
        h = h + 0.5 * swiglu(rms_norm(h, ffn2_norm[l]), ffn2_w_gate[l], ffn2_w_up[l], ffn2_w_down[l])
    return rms_norm(h, final_norm)[:, N_META:]
```

```python
import functools

import jax
import jax.numpy as jnp
from jax import lax
from jax.experimental import pallas as pl
from jax.experimental.pallas import tpu as pltpu

N_META = 16
HEAD_DIM = 128
META_BLOCK = 128
EPS = 1e-6
LANES = 128
VMEM_LIMIT = 56 * 1024 * 1024

F32 = jnp.float32
BF16 = jnp.bfloat16
NEG_INF = float("-inf")


def _dot(a, b):
    return jnp.dot(a, b, preferred_element_type=F32)


def _dot_nt(a, b):
    return lax.dot_general(a, b, (((1,), (1,)), ((), ())), preferred_element_type=F32)


def _rms(x, g):
    return x * lax.rsqrt(jnp.mean(x * x, axis=-1, keepdims=True) + EPS) * g


def _log_sigmoid_neg(z):
    return jnp.minimum(-z, 0.0) - jnp.log(1.0 + jnp.exp(-jnp.abs(z)))


def _head_norm(o, g):
    return o * lax.rsqrt(jnp.mean(o * o, axis=-1, keepdims=True) + EPS) * g


def _ffn_kernel(h_ref, g_ref, wg_ref, wu_ref, wd_ref, *rest, final):
    if final:
        gf_ref, o_ref, xn_ref = rest
    else:
        o_ref, xn_ref = rest
    j = pl.program_id(1)

    @pl.when(j == 0)
    def _():
        h = h_ref[...]
        xn_ref[...] = _rms(h, g_ref[...]).astype(BF16)
        o_ref[...] = h

    xn = xn_ref[...]
    gate = _dot(xn, wg_ref[...])
    up = _dot(xn, wu_ref[...])
    a = (gate * jax.nn.sigmoid(gate) * up * 0.5).astype(BF16)
    o_ref[...] += _dot(a, wd_ref[...])

    if final:
        @pl.when(j == pl.num_programs(1) - 1)
        def _():
            o_ref[...] = _rms(o_ref[...], gf_ref[...])


def _ffn(h, g, wg, wu, wd, *, tm, tf, final_gain=None, rows=None):
    m, d = h.shape
    f = wg.shape[1]
    rows = m if rows is None else rows
    final = final_gain is not None
    in_specs = [
        pl.BlockSpec((tm, d), lambda i, j: (i, 0)),
        pl.BlockSpec((1, d), lambda i, j: (0, 0)),
        pl.BlockSpec((d, tf), lambda i, j: (0, j)),
        pl.BlockSpec((d, tf), lambda i, j: (0, j)),
        pl.BlockSpec((tf, d), lambda i, j: (j, 0)),
    ]
    args = [h, g, wg, wu, wd]
    if final:
        in_specs.append(pl.BlockSpec((1, d), lambda i, j: (0, 0)))
        args.append(final_gain)
    return pl.pallas_call(
        functools.partial(_ffn_kernel, final=final),
        out_shape=jax.ShapeDtypeStruct((rows, d), F32),
        grid=(rows // tm, f // tf),
        in_specs=in_specs,
        out_specs=pl.BlockSpec((tm, d), lambda i, j: (i, 0)),
        scratch_shapes=[pltpu.VMEM((tm, d), BF16)],
        compiler_params=pltpu.CompilerParams(
            dimension_semantics=("arbitrary", "arbitrary"), vmem_limit_bytes=VMEM_LIMIT),
        name="ffn_final" if final else "ffn",
    )(*args)


def _proj_kernel(h_ref, g_ref, w_ref, wf_ref, qkv_ref, fl_ref, xn_ref, *, n_heads, q_scale):
    j = pl.program_id(1)

    @pl.when(j == 0)
    def _():
        xn = _rms(h_ref[...], g_ref[...]).astype(BF16)
        xn_ref[...] = xn
        fl_ref[...] = _dot_nt(wf_ref[...], xn)

    acc = _dot(xn_ref[...], w_ref[...])
    acc = acc * jnp.where((j == 0) | (j == 3), q_scale, 1.0)
    for hh in range(n_heads):
        qkv_ref[hh] = acc[:, hh * HEAD_DIM:(hh + 1) * HEAD_DIM].astype(BF16)


def _proj(h, g, w_qkv, wf_t, *, tm, n_heads):
    m, d = h.shape
    hd = n_heads * HEAD_DIM
    return pl.pallas_call(
        functools.partial(_proj_kernel, n_heads=n_heads, q_scale=HEAD_DIM ** -0.5),
        out_shape=(jax.ShapeDtypeStruct((6 * n_heads, m, HEAD_DIM), BF16),
                   jax.ShapeDtypeStruct((n_heads, m), F32)),
        grid=(m // tm, 6),
        in_specs=[
            pl.BlockSpec((tm, d), lambda i, j: (i, 0)),
            pl.BlockSpec((1, d), lambda i, j: (0, 0)),
            pl.BlockSpec((d, hd), lambda i, j: (0, j)),
            pl.BlockSpec((n_heads, d), lambda i, j: (0, 0)),
        ],
        out_specs=(pl.BlockSpec((n_heads, tm, HEAD_DIM), lambda i, j: (j, i, 0)),
                   pl.BlockSpec((n_heads, tm), lambda i, j: (0, i))),
        scratch_shapes=[pltpu.VMEM((tm, d), BF16)],
        compiler_params=pltpu.CompilerParams(
            dimension_semantics=("arbitrary", "arbitrary"), vmem_limit_bytes=VMEM_LIMIT),
        name="in_proj",
    )(h, g, w_qkv, wf_t)


def _scan_kernel(fl_ref, b_ref, nc_ref, *, batch, seq):
    n_heads = fl_ref.shape[0]
    lane = lax.broadcasted_iota(jnp.int32, (n_heads, LANES), 1)
    bias = b_ref[...]

    def log_f(v):
        v = v + bias
        return jnp.minimum(v, 0.0) - jnp.log(1.0 + jnp.exp(-jnp.abs(v)))

    def scan_lanes(v):
        shift = 1
        while shift < LANES:
            v = v + jnp.where(lane >= shift, pltpu.roll(v, shift, axis=1), 0.0)
            shift *= 2
        return v

    for b in range(batch):
        m0 = batch * seq + b * META_BLOCK
        lf = jnp.where(lane >= META_BLOCK - N_META, log_f(fl_ref[:, m0:m0 + META_BLOCK]), 0.0)
        c_meta = scan_lanes(lf)
        nc_ref[:, m0:m0 + META_BLOCK] = -c_meta

        def body(k, carry, b=b):
            start = pl.multiple_of(b * seq + k * LANES, LANES)
            c = scan_lanes(log_f(fl_ref[:, pl.ds(start, LANES)])) + carry
            nc_ref[:, pl.ds(start, LANES)] = -c
            return c[:, LANES - 1:LANES]

        lax.fori_loop(0, seq // LANES, body, c_meta[:, LANES - 1:LANES])


def _scan(fl, b_forget, *, batch, seq):
    n_heads, m = fl.shape
    return pl.pallas_call(
        functools.partial(_scan_kernel, batch=batch, seq=seq),
        out_shape=jax.ShapeDtypeStruct((n_heads, m), F32),
        name="forget_scan",
    )(fl, b_forget.reshape(n_heads, 1))


def _fox_kernel(q_ref, kx_ref, vx_ref, km_ref, vm_ref, ncx_ref, ncm_ref, g_ref, o_ref, *, tq, seq):
    row = lax.broadcasted_iota(jnp.int32, (tq, tq), 0)
    col = lax.broadcasted_iota(jnp.int32, (tq, tq), 1)
    causal = col <= row
    mcol = lax.broadcasted_iota(jnp.int32, (1, META_BLOCK), 1)
    meta_bias = jnp.where(mcol >= META_BLOCK - N_META, ncm_ref[...], NEG_INF)
    km = km_ref[...]
    vm = vm_ref[...]
    g = g_ref[...]

    def q_tile(t, _):
        q0 = pl.multiple_of(t * tq, tq)
        q = q_ref[pl.ds(q0, tq), :]
        s = _dot_nt(q, km) + meta_bias
        m = jnp.max(s, axis=-1, keepdims=True)
        p = jnp.exp(s - m)
        l = jnp.sum(p, axis=-1, keepdims=True)
        acc = _dot(p.astype(BF16), vm)

        def kv_step(k0, carry, masked):
            m, l, acc = carry
            k = kx_ref[pl.ds(k0, tq), :]
            v = vx_ref[pl.ds(k0, tq), :]
            s = _dot_nt(q, k) + ncx_ref[:, pl.ds(k0, tq)]
            if masked:
                s = jnp.where(causal, s, NEG_INF)
            m_new = jnp.maximum(m, jnp.max(s, axis=-1, keepdims=True))
            alpha = jnp.exp(m - m_new)
            p = jnp.exp(s - m_new)
            l = alpha * l + jnp.sum(p, axis=-1, keepdims=True)
            acc = alpha * acc + _dot(p.astype(BF16), v)
            return m_new, l, acc

        carry = lax.fori_loop(
            0, t, lambda j, c: kv_step(pl.multiple_of(j * tq, tq), c, False), (m, l, acc))
        m, l, acc = kv_step(q0, carry, True)
        o_ref[pl.ds(q0, tq), :] = _head_norm(acc / l, g).astype(BF16)
        return 0

    lax.fori_loop(0, seq // tq, q_tile, 0)


def _sb_block(q, k, v, r, acc, mask, upper):
    z = _dot_nt(q, k)
    l1m = _log_sigmoid_neg(z)
    if mask is not None:
        l1m = jnp.where(mask, l1m, 0.0)
    hi = l1m.astype(BF16)
    lo = (l1m - hi.astype(F32)).astype(BF16)
    later = _dot(hi, upper) + _dot(lo, upper) + r
    e = z + l1m + later
    if mask is not None:
        e = jnp.where(mask, e, NEG_INF)
    acc = acc + _dot(jnp.exp(e).astype(BF16), v)
    r = later[:, :1] + l1m[:, :1]
    return r, acc


def _sb_kernel(q_ref, kx_ref, vx_ref, km_ref, vm_ref, g_ref, o_ref, *, tq, seq):
    row = lax.broadcasted_iota(jnp.int32, (tq, tq), 0)
    col = lax.broadcasted_iota(jnp.int32, (tq, tq), 1)
    strict = col < row
    upper = (row > col).astype(BF16)
    mrow = lax.broadcasted_iota(jnp.int32, (META_BLOCK, META_BLOCK), 0)
    mcol = lax.broadcasted_iota(jnp.int32, (META_BLOCK, META_BLOCK), 1)
    upper_m = (mrow > mcol).astype(BF16)
    meta_valid = lax.broadcasted_iota(jnp.int32, (1, META_BLOCK), 1) >= META_BLOCK - N_META
    km = km_ref[...]
    vm = vm_ref[...]
    g = g_ref[...]

    def q_tile(t, _):
        q0 = pl.multiple_of(t * tq, tq)
        q = q_ref[pl.ds(q0, tq), :]
        r = jnp.zeros((tq, 1), F32)
        acc = jnp.zeros((tq, HEAD_DIM), F32)
        r, acc = _sb_block(q, kx_ref[pl.ds(q0, tq), :], vx_ref[pl.ds(q0, tq), :], r, acc, strict, upper)

        def kv_step(jj, carry):
            r, acc = carry
            k0 = pl.multiple_of((t - 1 - jj) * tq, tq)
            return _sb_block(q, kx_ref[pl.ds(k0, tq), :], vx_ref[pl.ds(k0, tq), :], r, acc, None, upper)

        r, acc = lax.fori_loop(0, t, kv_step, (r, acc))
        r, acc = _sb_block(q, km, vm, r, acc, meta_valid, upper_m)
        o_ref[pl.ds(q0, tq), :] = _head_norm(acc, g).astype(BF16)
        return 0

    lax.fori_loop(0, seq // tq, q_tile, 0)


def _attention(qkv, nc, g_fox, g_sb, *, batch, seq, n_heads, tq):
    H = n_heads
    hd = H * HEAD_DIM
    meta0 = batch * seq // META_BLOCK
    nc3 = nc.reshape(H, 1, nc.shape[1])

    def tok(base):
        return pl.BlockSpec((None, seq, HEAD_DIM), lambda b, h: (base + h, b, 0))

    def meta(base):
        return pl.BlockSpec((None, META_BLOCK, HEAD_DIM), lambda b, h: (base + h, meta0 + b, 0))

    gain = pl.BlockSpec((1, HEAD_DIM), lambda b, h: (0, h))
    out_spec = pl.BlockSpec((seq, HEAD_DIM), lambda b, h: (b, h))
    out_shape = jax.ShapeDtypeStruct((batch * seq, hd), BF16)
    params = pltpu.CompilerParams(
        dimension_semantics=("arbitrary", "arbitrary"), vmem_limit_bytes=VMEM_LIMIT)

    o_f = pl.pallas_call(
        functools.partial(_fox_kernel, tq=tq, seq=seq),
        out_shape=out_shape,
        grid=(batch, H),
        in_specs=[tok(0), tok(H), tok(2 * H), meta(H), meta(2 * H),
                  pl.BlockSpec((None, 1, seq), lambda b, h: (h, 0, b)),
                  pl.BlockSpec((None, 1, META_BLOCK), lambda b, h: (h, 0, meta0 + b)),
                  gain],
        out_specs=out_spec,
        compiler_params=params,
        name="fox_attention",
    )(qkv, qkv, qkv, qkv, qkv, nc3, nc3, g_fox)

    o_s = pl.pallas_call(
        functools.partial(_sb_kernel, tq=tq, seq=seq),
        out_shape=out_shape,
        grid=(batch, H),
        in_specs=[tok(3 * H), tok(4 * H), tok(5 * H), meta(4 * H), meta(5 * H), gain],
        out_specs=out_spec,
        compiler_params=params,
        name="sb_attention",
    )(qkv, qkv, qkv, qkv, qkv, g_sb)
    return o_f, o_s


def _meta_attn_kernel(q_ref, k_ref, v_ref, ncm_ref, g_ref, o_ref, *, n_heads):
    grp = pl.program_id(1)
    row = lax.broadcasted_iota(jnp.int32, (META_BLOCK, META_BLOCK), 0)
    col = lax.broadcasted_iota(jnp.int32, (META_BLOCK, META_BLOCK), 1)
    valid = col >= META_BLOCK - N_META
    q = q_ref[...]
    k = k_ref[...]
    v = v_ref[...]
    g = g_ref[...]

    @pl.when(grp < n_heads)
    def _():
        allowed = (col <= row) & (valid | (col == row))
        s = jnp.where(allowed, _dot_nt(q, k) + ncm_ref[...], NEG_INF)
        p = jnp.exp(s - jnp.max(s, axis=-1, keepdims=True))
        o = _dot(p.astype(BF16), v) / jnp.sum(p, axis=-1, keepdims=True)
        o_ref[...] = _head_norm(o, g).astype(BF16)

    @pl.when(grp >= n_heads)
    def _():
        upper = (row > col).astype(BF16)
        _, o = _sb_block(q, k, v, jnp.zeros((META_BLOCK, 1), F32), jnp.zeros((META_BLOCK, HEAD_DIM), F32),
                         (col < row) & valid, upper)
        o_ref[...] = _head_norm(o, g).astype(BF16)


def _meta_attention(qkv, nc, g_cat, *, batch, seq, n_heads):
    H = n_heads
    meta0 = batch * seq // META_BLOCK
    nc3 = nc.reshape(H, 1, nc.shape[1])

    def head_block(fox_base, sb_base):
        return pl.BlockSpec(
            (None, META_BLOCK, HEAD_DIM),
            lambda b, g: (jnp.where(g < H, fox_base + g, sb_base + g - H), meta0 + b, 0))

    return pl.pallas_call(
        functools.partial(_meta_attn_kernel, n_heads=H),
        out_shape=jax.ShapeDtypeStruct((batch * META_BLOCK, 2 * H * HEAD_DIM), BF16),
        grid=(batch, 2 * H),
        in_specs=[head_block(0, 3 * H), head_block(H, 4 * H), head_block(2 * H, 5 * H),
                  pl.BlockSpec((None, 1, META_BLOCK), lambda b, g: (jnp.minimum(g, H - 1), 0, meta0 + b)),
                  pl.BlockSpec((1, HEAD_DIM), lambda b, g: (0, g))],
        out_specs=pl.BlockSpec((META_BLOCK, HEAD_DIM), lambda b, g: (b, g)),
        compiler_params=pltpu.CompilerParams(dimension_semantics=("arbitrary", "arbitrary")),
        name="meta_attention",
    )(qkv, qkv, qkv, nc3, g_cat)


def _out_proj_kernel(h_ref, of_ref, os_ref, om_ref, w_ref, o_ref, *, n_tok_tiles, hd):
    i = pl.program_id(0)

    @pl.when(i < n_tok_tiles)
    def _():
        o_ref[...] = h_ref[...] + _dot(of_ref[...], w_ref[:hd, :]) + _dot(os_ref[...], w_ref[hd:, :])

    @pl.when(i >= n_tok_tiles)
    def _():
        o_ref[...] = h_ref[...] + _dot(om_ref[...], w_ref[...])


def _out_proj(h, o_f, o_s, o_m, w, *, tm):
    m, d = h.shape
    hd = o_f.shape[1]
    n_tok_tiles = o_f.shape[0] // tm
    assert o_m.shape[0] == tm and m == (n_tok_tiles + 1) * tm
    last = n_tok_tiles - 1
    return pl.pallas_call(
        functools.partial(_out_proj_kernel, n_tok_tiles=n_tok_tiles, hd=hd),
        out_shape=jax.ShapeDtypeStruct((m, d), F32),
        grid=(m // tm,),
        in_specs=[
            pl.BlockSpec((tm, d), lambda i: (i, 0)),
            pl.BlockSpec((tm, hd), lambda i: (jnp.minimum(i, last), 0)),
            pl.BlockSpec((tm, hd), lambda i: (jnp.minimum(i, last), 0)),
            pl.BlockSpec((tm, 2 * hd), lambda i: (0, 0)),
            pl.BlockSpec((2 * hd, d), lambda i: (0, 0)),
        ],
        out_specs=pl.BlockSpec((tm, d), lambda i: (i, 0)),
        compiler_params=pltpu.CompilerParams(
            dimension_semantics=("arbitrary",), vmem_limit_bytes=VMEM_LIMIT),
        name="out_proj",
    )(h, o_f, o_s, o_m, w)


def kernel(x, meta_tokens, ffn1_norm, ffn1_w_gate, ffn1_w_up, ffn1_w_down, mix_norm, w_in, b_forget, g_fox, g_sb, w_out, ffn2_norm, ffn2_w_gate, ffn2_w_up, ffn2_w_down, final_norm):
    batch, seq, d = x.shape
    depth = w_in.shape[0]
    n_heads = g_fox.shape[1] // HEAD_DIM
    hd = n_heads * HEAD_DIM
    f = ffn1_w_gate.shape[2]
    assert meta_tokens.shape[0] == N_META and w_in.shape[2] == 6 * hd + n_heads

    tm = batch * META_BLOCK
    assert (batch * seq) % tm == 0
    tf = 512 if f % 512 == 0 else LANES
    tq = 256 if seq % 256 == 0 else LANES

    lead = jnp.concatenate([jnp.zeros((META_BLOCK - N_META, d), x.dtype), meta_tokens.astype(x.dtype)], axis=0)
    h = jnp.concatenate([x.reshape(batch * seq, d), jnp.tile(lead, (batch, 1))], axis=0)

    row = lambda v: v.reshape(1, -1)
    for l in range(depth):
        h = _ffn(h, row(ffn1_norm[l]), ffn1_w_gate[l].astype(BF16), ffn1_w_up[l].astype(BF16),
                 ffn1_w_down[l].astype(BF16), tm=tm, tf=tf)

        qkv, fl = _proj(h, row(mix_norm[l]), w_in[l, :, :6 * hd].astype(BF16),
                        w_in[l, :, 6 * hd:].T.astype(BF16), tm=tm, n_heads=n_heads)
        nc = _scan(fl, b_forget[l], batch=batch, seq=seq)
        o_f, o_s = _attention(qkv, nc, row(g_fox[l]), row(g_sb[l]),
                              batch=batch, seq=seq, n_heads=n_heads, tq=tq)
        o_m = _meta_attention(qkv, nc, row(jnp.concatenate([g_fox[l], g_sb[l]])),
                              batch=batch, seq=seq, n_heads=n_heads)
        h = _out_proj(h, o_f, o_s, o_m, w_out[l].astype(BF16), tm=tm)

        last = l == depth - 1
        h = _ffn(h, row(ffn2_norm[l]), ffn2_w_gate[l].astype(BF16), ffn2_w_up[l].astype(BF16),
                 ffn2_w_down[l].astype(BF16), tm=tm, tf=tf,
                 final_gain=row(final_norm) if last else None, rows=batch * seq if last else None)
    return h.reshape(batch, seq, d)
```

```python
import functools

import jax
import jax.numpy as jnp
from jax import lax
from jax.experimental import pallas as pl
from jax.experimental.pallas import tpu as pltpu

N_META = 16
HEAD_DIM = 128
META_BLOCK = 128
EPS = 1e-6
LANES = 128
SB_KEYS = 256
LOG2E = 1.4426950408889634
VMEM_LIMIT = 56 * 1024 * 1024

F32 = jnp.float32
BF16 = jnp.bfloat16
NEG_INF = float("-inf")


def _dot(a, b):
    return jnp.dot(a, b, preferred_element_type=F32)


def _dot_nt(a, b):
    return lax.dot_general(a, b, (((1,), (1,)), ((), ())), preferred_element_type=F32)


def _rms(x, g):
    return x * lax.rsqrt(jnp.mean(x * x, axis=-1, keepdims=True) + EPS) * g


def _head_norm(o, g):
    return o * lax.rsqrt(jnp.mean(o * o, axis=-1, keepdims=True) + EPS) * g


def _ffn_kernel(h_ref, g_ref, wg_ref, wu_ref, wd_ref, *rest, final):
    if final:
        gf_ref, o_ref, xn_ref = rest
    else:
        o_ref, xn_ref = rest
    j = pl.program_id(1)

    @pl.when(j == 0)
    def _():
        h = h_ref[...]
        xn_ref[...] = _rms(h, g_ref[...]).astype(BF16)
        o_ref[...] = h

    xn = xn_ref[...]
    gate = _dot(xn, wg_ref[...])
    up = _dot(xn, wu_ref[...])
    a = (gate * jax.nn.sigmoid(gate) * up * 0.5).astype(BF16)
    o_ref[...] += _dot(a, wd_ref[...])

    if final:
        @pl.when(j == pl.num_programs(1) - 1)
        def _():
            o_ref[...] = _rms(o_ref[...], gf_ref[...])


def _ffn(h, g, wg, wu, wd, *, tm, tf, final_gain=None, rows=None):
    m, d = h.shape
    f = wg.shape[1]
    rows = m if rows is None else rows
    final = final_gain is not None
    in_specs = [
        pl.BlockSpec((tm, d), lambda i, j: (i, 0)),
        pl.BlockSpec((1, d), lambda i, j: (0, 0)),
        pl.BlockSpec((d, tf), lambda i, j: (0, j)),
        pl.BlockSpec((d, tf), lambda i, j: (0, j)),
        pl.BlockSpec((tf, d), lambda i, j: (j, 0)),
    ]
    args = [h, g, wg, wu, wd]
    if final:
        in_specs.append(pl.BlockSpec((1, d), lambda i, j: (0, 0)))
        args.append(final_gain)
    return pl.pallas_call(
        functools.partial(_ffn_kernel, final=final),
        out_shape=jax.ShapeDtypeStruct((rows, d), F32),
        grid=(rows // tm, f // tf),
        in_specs=in_specs,
        out_specs=pl.BlockSpec((tm, d), lambda i, j: (i, 0)),
        scratch_shapes=[pltpu.VMEM((tm, d), BF16)],
        compiler_params=pltpu.CompilerParams(
            dimension_semantics=("arbitrary", "arbitrary"), vmem_limit_bytes=VMEM_LIMIT),
        name="ffn_final" if final else "ffn",
    )(*args)


def _proj_kernel(h_ref, g_ref, w_ref, wf_ref, qkv_ref, fl_ref, xn_ref, *, n_heads, q_scale):
    j = pl.program_id(1)

    @pl.when(j == 0)
    def _():
        xn = _rms(h_ref[...], g_ref[...]).astype(BF16)
        xn_ref[...] = xn
        fl_ref[...] = _dot_nt(wf_ref[...], xn)

    acc = _dot(xn_ref[...], w_ref[...])
    acc = acc * jnp.where((j == 0) | (j == 3), q_scale, 1.0)
    for hh in range(n_heads):
        qkv_ref[hh] = acc[:, hh * HEAD_DIM:(hh + 1) * HEAD_DIM].astype(BF16)


def _proj(h, g, w_qkv, wf_t, *, tm, n_heads):
    m, d = h.shape
    hd = n_heads * HEAD_DIM
    return pl.pallas_call(
        functools.partial(_proj_kernel, n_heads=n_heads, q_scale=LOG2E * HEAD_DIM ** -0.5),
        out_shape=(jax.ShapeDtypeStruct((6 * n_heads, m, HEAD_DIM), BF16),
                   jax.ShapeDtypeStruct((n_heads, m), F32)),
        grid=(m // tm, 6),
        in_specs=[
            pl.BlockSpec((tm, d), lambda i, j: (i, 0)),
            pl.BlockSpec((1, d), lambda i, j: (0, 0)),
            pl.BlockSpec((d, hd), lambda i, j: (0, j)),
            pl.BlockSpec((n_heads, d), lambda i, j: (0, 0)),
        ],
        out_specs=(pl.BlockSpec((n_heads, tm, HEAD_DIM), lambda i, j: (j, i, 0)),
                   pl.BlockSpec((n_heads, tm), lambda i, j: (0, i))),
        scratch_shapes=[pltpu.VMEM((tm, d), BF16)],
        compiler_params=pltpu.CompilerParams(
            dimension_semantics=("arbitrary", "arbitrary"), vmem_limit_bytes=VMEM_LIMIT),
        name="in_proj",
    )(h, g, w_qkv, wf_t)


def _scan_kernel(fl_ref, b_ref, nc_ref, *, batch, seq):
    n_heads = fl_ref.shape[0]
    lane = lax.broadcasted_iota(jnp.int32, (n_heads, LANES), 1)
    bias = b_ref[...]

    def log_f(v):
        v = v + bias
        return jnp.minimum(v, 0.0) - jnp.log(1.0 + jnp.exp(-jnp.abs(v)))

    def scan_lanes(v):
        shift = 1
        while shift < LANES:
            v = v + jnp.where(lane >= shift, pltpu.roll(v, shift, axis=1), 0.0)
            shift *= 2
        return v

    for b in range(batch):
        m0 = batch * seq + b * META_BLOCK
        lf = jnp.where(lane >= META_BLOCK - N_META, log_f(fl_ref[:, m0:m0 + META_BLOCK]), 0.0)
        c_meta = scan_lanes(lf)
        nc_ref[:, m0:m0 + META_BLOCK] = -LOG2E * c_meta

        def body(k, carry, b=b):
            start = pl.multiple_of(b * seq + k * LANES, LANES)
            c = scan_lanes(log_f(fl_ref[:, pl.ds(start, LANES)])) + carry
            nc_ref[:, pl.ds(start, LANES)] = -LOG2E * c
            return c[:, LANES - 1:LANES]

        lax.fori_loop(0, seq // LANES, body, c_meta[:, LANES - 1:LANES])


def _scan(fl, b_forget, *, batch, seq):
    n_heads, m = fl.shape
    return pl.pallas_call(
        functools.partial(_scan_kernel, batch=batch, seq=seq),
        out_shape=jax.ShapeDtypeStruct((n_heads, m), F32),
        name="forget_scan",
    )(fl, b_forget.reshape(n_heads, 1))


def _fox_step(q, k, v, nc, carry, mask):
    m, l, acc = carry
    s = _dot_nt(q, k) + nc
    if mask is not None:
        s = jnp.where(mask, s, NEG_INF)
    m_new = jnp.maximum(m, jnp.max(s, axis=-1, keepdims=True))
    alpha = jnp.exp2(m - m_new)
    p = jnp.exp2(s - m_new)
    l = alpha * l + jnp.sum(p, axis=-1, keepdims=True)
    acc = alpha * acc + _dot(p.astype(BF16), v)
    return m_new, l, acc


def _sb_block(q, k, v, r, acc, mask, upper):
    z = _dot_nt(q, k)
    l1m = jnp.minimum(-z, 0.0) - jnp.log(1.0 + jnp.exp2(-jnp.abs(z))) * LOG2E
    if mask is not None:
        l1m = jnp.where(mask, l1m, 0.0)
    hi = l1m.astype(BF16)
    lo = (l1m - hi.astype(F32)).astype(BF16)
    later = _dot(hi, upper) + _dot(lo, upper) + r
    e = z + l1m + later
    if mask is not None:
        e = jnp.where(mask, e, NEG_INF)
    acc = acc + _dot(jnp.exp2(e).astype(BF16), v)
    r = later[:, :1] + l1m[:, :1]
    return r, acc


def _mixer_kernel(qf_ref, kf_ref, vf_ref, qs_ref, ks_ref, vs_ref, kfm_ref, vfm_ref, ksm_ref, vsm_ref,
                  ncx_ref, ncm_ref, gf_ref, gs_ref, of_ref, os_ref, *, tq, seq):
    n_sub = tq // SB_KEYS
    row = lax.broadcasted_iota(jnp.int32, (tq, SB_KEYS), 0)
    col = lax.broadcasted_iota(jnp.int32, (tq, SB_KEYS), 1)
    causal = col <= row
    strict = col < row
    urow = lax.broadcasted_iota(jnp.int32, (SB_KEYS, SB_KEYS), 0)
    ucol = lax.broadcasted_iota(jnp.int32, (SB_KEYS, SB_KEYS), 1)
    upper = (urow > ucol).astype(BF16)
    mrow = lax.broadcasted_iota(jnp.int32, (META_BLOCK, META_BLOCK), 0)
    mcol = lax.broadcasted_iota(jnp.int32, (META_BLOCK, META_BLOCK), 1)
    upper_m = (mrow > mcol).astype(BF16)
    meta_valid = lax.broadcasted_iota(jnp.int32, (1, META_BLOCK), 1) >= META_BLOCK - N_META
    meta_bias = jnp.where(meta_valid, ncm_ref[...], NEG_INF)

    def q_tile(t, _):
        q0 = pl.multiple_of(t * tq, tq)
        qf = qf_ref[pl.ds(q0, tq), :]
        qs = qs_ref[pl.ds(q0, tq), :]

        s = _dot_nt(qf, kfm_ref[...]) + meta_bias
        m = jnp.max(s, axis=-1, keepdims=True)
        p = jnp.exp2(s - m)
        fox = (m, jnp.sum(p, axis=-1, keepdims=True), _dot(p.astype(BF16), vfm_ref[...]))

        sb = None
        for d in reversed(range(n_sub)):
            r0 = d * SB_KEYS
            k0 = pl.multiple_of(q0 + r0, SB_KEYS)
            if sb is None:
                r, acc = jnp.zeros((tq - r0, 1), F32), jnp.zeros((tq - r0, HEAD_DIM), F32)
            else:
                r = jnp.concatenate([jnp.zeros((SB_KEYS, 1), F32), sb[0]], axis=0)
                acc = jnp.concatenate([jnp.zeros((SB_KEYS, HEAD_DIM), F32), sb[1]], axis=0)
            sb = _sb_block(qs[r0:], ks_ref[pl.ds(k0, SB_KEYS), :], vs_ref[pl.ds(k0, SB_KEYS), :],
                           r, acc, strict[:tq - r0], upper)

        def kv_step(jj, carry):
            fox, sb = carry
            kf0 = pl.multiple_of(jj * tq, tq)
            fox = _fox_step(qf, kf_ref[pl.ds(kf0, tq), :], vf_ref[pl.ds(kf0, tq), :],
                            ncx_ref[:, pl.ds(kf0, tq)], fox, None)
            for d in reversed(range(n_sub)):
                k0 = pl.multiple_of((t - 1 - jj) * tq + d * SB_KEYS, SB_KEYS)
                sb = _sb_block(qs, ks_ref[pl.ds(k0, SB_KEYS), :], vs_ref[pl.ds(k0, SB_KEYS), :],
                               sb[0], sb[1], None, upper)
            return fox, sb

        fox, sb = lax.fori_loop(0, t, kv_step, (fox, sb))

        for d in range(n_sub):
            r0 = d * SB_KEYS
            k0 = pl.multiple_of(q0 + r0, SB_KEYS)
            sub = _fox_step(qf[r0:], kf_ref[pl.ds(k0, SB_KEYS), :], vf_ref[pl.ds(k0, SB_KEYS), :],
                            ncx_ref[:, pl.ds(k0, SB_KEYS)], tuple(a[r0:] for a in fox), causal[:tq - r0])
            fox = tuple(jnp.concatenate([a[:r0], b], axis=0) for a, b in zip(fox, sub)) if r0 else sub
        sb = _sb_block(qs, ksm_ref[...], vsm_ref[...], sb[0], sb[1], meta_valid, upper_m)

        of_ref[pl.ds(q0, tq), :] = _head_norm(fox[2] / fox[1], gf_ref[...]).astype(BF16)
        os_ref[pl.ds(q0, tq), :] = _head_norm(sb[1], gs_ref[...]).astype(BF16)
        return 0

    lax.fori_loop(0, seq // tq, q_tile, 0)


def _attention(qkv, nc, g_fox, g_sb, *, batch, seq, n_heads, tq):
    H = n_heads
    hd = H * HEAD_DIM
    meta0 = batch * seq // META_BLOCK
    nc3 = nc.reshape(H, 1, nc.shape[1])

    def tok(base):
        return pl.BlockSpec((None, seq, HEAD_DIM), lambda b, h: (base + h, b, 0))

    def meta(base):
        return pl.BlockSpec((None, META_BLOCK, HEAD_DIM), lambda b, h: (base + h, meta0 + b, 0))

    gain = pl.BlockSpec((1, HEAD_DIM), lambda b, h: (0, h))
    out_spec = pl.BlockSpec((seq, HEAD_DIM), lambda b, h: (b, h))
    out_shape = jax.ShapeDtypeStruct((batch * seq, hd), BF16)
    return pl.pallas_call(
        functools.partial(_mixer_kernel, tq=tq, seq=seq),
        out_shape=(out_shape, out_shape),
        grid=(batch, H),
        in_specs=[tok(0), tok(H), tok(2 * H), tok(3 * H), tok(4 * H), tok(5 * H),
                  meta(H), meta(2 * H), meta(4 * H), meta(5 * H),
                  pl.BlockSpec((None, 1, seq), lambda b, h: (h, 0, b)),
                  pl.BlockSpec((None, 1, META_BLOCK), lambda b, h: (h, 0, meta0 + b)),
                  gain, gain],
        out_specs=(out_spec, out_spec),
        compiler_params=pltpu.CompilerParams(
            dimension_semantics=("arbitrary", "arbitrary"), vmem_limit_bytes=VMEM_LIMIT),
        name="mixer_attention",
    )(qkv, qkv, qkv, qkv, qkv, qkv, qkv, qkv, qkv, qkv, nc3, nc3, g_fox, g_sb)


def _meta_attn_kernel(q_ref, k_ref, v_ref, ncm_ref, g_ref, o_ref, *, n_heads):
    grp = pl.program_id(1)
    row = lax.broadcasted_iota(jnp.int32, (META_BLOCK, META_BLOCK), 0)
    col = lax.broadcasted_iota(jnp.int32, (META_BLOCK, META_BLOCK), 1)
    valid = col >= META_BLOCK - N_META
    q = q_ref[...]
    k = k_ref[...]
    v = v_ref[...]
    g = g_ref[...]

    @pl.when(grp < n_heads)
    def _():
        allowed = (col <= row) & (valid | (col == row))
        s = jnp.where(allowed, _dot_nt(q, k) + ncm_ref[...], NEG_INF)
        p = jnp.exp2(s - jnp.max(s, axis=-1, keepdims=True))
        o = _dot(p.astype(BF16), v) / jnp.sum(p, axis=-1, keepdims=True)
        o_ref[...] = _head_norm(o, g).astype(BF16)

    @pl.when(grp >= n_heads)
    def _():
        upper = (row > col).astype(BF16)
        _, o = _sb_block(q, k, v, jnp.zeros((META_BLOCK, 1), F32), jnp.zeros((META_BLOCK, HEAD_DIM), F32),
                         (col < row) & valid, upper)
        o_ref[...] = _head_norm(o, g).astype(BF16)


def _meta_attention(qkv, nc, g_cat, *, batch, seq, n_heads):
    H = n_heads
    meta0 = batch * seq // META_BLOCK
    nc3 = nc.reshape(H, 1, nc.shape[1])

    def head_block(fox_base, sb_base):
        return pl.BlockSpec(
            (None, META_BLOCK, HEAD_DIM),
            lambda b, g: (jnp.where(g < H, fox_base + g, sb_base + g - H), meta0 + b, 0))

    return pl.pallas_call(
        functools.partial(_meta_attn_kernel, n_heads=H),
        out_shape=jax.ShapeDtypeStruct((batch * META_BLOCK, 2 * H * HEAD_DIM), BF16),
        grid=(batch, 2 * H),
        in_specs=[head_block(0, 3 * H), head_block(H, 4 * H), head_block(2 * H, 5 * H),
                  pl.BlockSpec((None, 1, META_BLOCK), lambda b, g: (jnp.minimum(g, H - 1), 0, meta0 + b)),
                  pl.BlockSpec((1, HEAD_DIM), lambda b, g: (0, g))],
        out_specs=pl.BlockSpec((META_BLOCK, HEAD_DIM), lambda b, g: (b, g)),
        compiler_params=pltpu.CompilerParams(dimension_semantics=("arbitrary", "arbitrary")),
        name="meta_attention",
    )(qkv, qkv, qkv, nc3, g_cat)


def _out_proj_kernel(h_ref, of_ref, os_ref, om_ref, w_ref, o_ref, *, n_tok_tiles, hd):
    i = pl.program_id(0)

    @pl.when(i < n_tok_tiles)
    def _():
        o_ref[...] = h_ref[...] + _dot(of_ref[...], w_ref[:hd, :]) + _dot(os_ref[...], w_ref[hd:, :])

    @pl.when(i >= n_tok_tiles)
    def _():
        o_ref[...] = h_ref[...] + _dot(om_ref[...], w_ref[...])


def _out_proj(h, o_f, o_s, o_m, w, *, tm):
    m, d = h.shape
    hd = o_f.shape[1]
    n_tok_tiles = o_f.shape[0] // tm
    assert o_m.shape[0] == tm and m == (n_tok_tiles + 1) * tm
    last = n_tok_tiles - 1
    return pl.pallas_call(
        functools.partial(_out_proj_kernel, n_tok_tiles=n_tok_tiles, hd=hd),
        out_shape=jax.ShapeDtypeStruct((m, d), F32),
        grid=(m // tm,),
        in_specs=[
            pl.BlockSpec((tm, d), lambda i: (i, 0)),
            pl.BlockSpec((tm, hd), lambda i: (jnp.minimum(i, last), 0)),
            pl.BlockSpec((tm, hd), lambda i: (jnp.minimum(i, last), 0)),
            pl.BlockSpec((tm, 2 * hd), lambda i: (0, 0)),
            pl.BlockSpec((2 * hd, d), lambda i: (0, 0)),
        ],
        out_specs=pl.BlockSpec((tm, d), lambda i: (i, 0)),
        compiler_params=pltpu.CompilerParams(
            dimension_semantics=("arbitrary",), vmem_limit_bytes=VMEM_LIMIT),
        name="out_proj",
    )(h, o_f, o_s, o_m, w)


def kernel(x, meta_tokens, ffn1_norm, ffn1_w_gate, ffn1_w_up, ffn1_w_down, mix_norm, w_in, b_forget, g_fox, g_sb, w_out, ffn2_norm, ffn2_w_gate, ffn2_w_up, ffn2_w_down, final_norm):
    batch, seq, d = x.shape
    depth = w_in.shape[0]
    n_heads = g_fox.shape[1] // HEAD_DIM
    hd = n_heads * HEAD_DIM
    f = ffn1_w_gate.shape[2]
    assert meta_tokens.shape[0] == N_META and w_in.shape[2] == 6 * hd + n_heads

    tm = batch * META_BLOCK
    assert (batch * seq) % tm == 0
    tf = 512 if f % 512 == 0 else LANES
    tq = 512
    assert seq % tq == 0

    lead = jnp.concatenate([jnp.zeros((META_BLOCK - N_META, d), x.dtype), meta_tokens.astype(x.dtype)], axis=0)
    h = jnp.concatenate([x.reshape(batch * seq, d), jnp.tile(lead, (batch, 1))], axis=0)

    row = lambda v: v.reshape(1, -1)
    for l in range(depth):
        h = _ffn(h, row(ffn1_norm[l]), ffn1_w_gate[l].astype(BF16), ffn1_w_up[l].astype(BF16),
                 ffn1_w_down[l].astype(BF16), tm=tm, tf=tf)

        qkv, fl = _proj(h, row(mix_norm[l]), w_in[l, :, :6 * hd].astype(BF16),
                        w_in[l, :, 6 * hd:].T.astype(BF16), tm=tm, n_heads=n_heads)
        nc = _scan(fl, b_forget[l], batch=batch, seq=seq)
        o_f, o_s = _attention(qkv, nc, row(g_fox[l]), row(g_sb[l]),
                              batch=batch, seq=seq, n_heads=n_heads, tq=tq)
        o_m = _meta_attention(qkv, nc, row(jnp.concatenate([g_fox[l], g_sb[l]])),
                              batch=batch, seq=seq, n_heads=n_heads)
        h = _out_proj(h, o_f, o_s, o_m, w_out[l].astype(BF16), tm=tm)

        last = l == depth - 1
        h = _ffn(h, row(ffn2_norm[l]), ffn2_w_gate[l].astype(BF16), ffn2_w_up[l].astype(BF16),
                 ffn2_w_down[l].astype(BF16), tm=tm, tf=tf,
                 final_gain=row(final_norm) if last else None, rows=batch * seq if last else None)
    return h.reshape(batch, seq, d)
```

```python
import functools

import jax
import jax.numpy as jnp
from jax import lax
from jax.experimental import pallas as pl
from jax.experimental.pallas import tpu as pltpu

N_META = 16
HEAD_DIM = 128
META_BLOCK = 128
EPS = 1e-6
LANES = 128
SB_KEYS = 256
LOG2E = 1.4426950408889634
VMEM_LIMIT = 56 * 1024 * 1024

F32 = jnp.float32
BF16 = jnp.bfloat16
NEG_INF = float("-inf")


def _dot(a, b):
    return jnp.dot(a, b, preferred_element_type=F32)


def _dot_nt(a, b):
    return lax.dot_general(a, b, (((1,), (1,)), ((), ())), preferred_element_type=F32)


def _rms(x, g):
    return x * lax.rsqrt(jnp.mean(x * x, axis=-1, keepdims=True) + EPS) * g


def _head_norm(o, g):
    return o * lax.rsqrt(jnp.mean(o * o, axis=-1, keepdims=True) + EPS) * g


def _ffn_kernel(h_ref, *rest, final, n_tok_tiles):
    lead_ref = None
    if n_tok_tiles is not None:
        lead_ref, *rest = rest
    g_ref, wg_ref, wu_ref, wd_ref, *rest = rest
    if final:
        gf_ref, o_ref, xn_ref = rest
    else:
        o_ref, xn_ref = rest
    i = pl.program_id(0)
    j = pl.program_id(1)

    def start_tile(src_ref):
        h = src_ref[...]
        xn_ref[...] = _rms(h, g_ref[...]).astype(BF16)
        o_ref[...] = h

    if lead_ref is None:
        pl.when(j == 0)(lambda: start_tile(h_ref))
    else:
        pl.when((j == 0) & (i < n_tok_tiles))(lambda: start_tile(h_ref))
        pl.when((j == 0) & (i >= n_tok_tiles))(lambda: start_tile(lead_ref))

    xn = xn_ref[...]
    gate = _dot(xn, wg_ref[...])
    up = _dot(xn, wu_ref[...])
    a = (gate * jax.nn.sigmoid(gate) * up * 0.5).astype(BF16)
    o_ref[...] += _dot(a, wd_ref[...])

    if final:
        @pl.when(j == pl.num_programs(1) - 1)
        def _():
            o_ref[...] = _rms(o_ref[...], gf_ref[...])


def _ffn(h, g, wg, wu, wd, *, tm, tf, final_gain=None, rows=None, lead=None):
    m, d = h.shape
    f = wg.shape[1]
    final = final_gain is not None
    n_tok_tiles = None
    if lead is None:
        rows = m if rows is None else rows
        in_specs = [pl.BlockSpec((tm, d), lambda i, j: (i, 0))]
        args = [h]
    else:
        assert lead.shape == (tm, d) and m % tm == 0 and rows is None
        n_tok_tiles = m // tm
        rows = m + tm
        in_specs = [pl.BlockSpec((tm, d), lambda i, j: (jnp.minimum(i, n_tok_tiles - 1), 0)),
                    pl.BlockSpec((tm, d), lambda i, j: (0, 0))]
        args = [h, lead]
    in_specs += [
        pl.BlockSpec((1, d), lambda i, j: (0, 0)),
        pl.BlockSpec((d, tf), lambda i, j: (0, j)),
        pl.BlockSpec((d, tf), lambda i, j: (0, j)),
        pl.BlockSpec((tf, d), lambda i, j: (j, 0)),
    ]
    args += [g, wg, wu, wd]
    if final:
        in_specs.append(pl.BlockSpec((1, d), lambda i, j: (0, 0)))
        args.append(final_gain)
    return pl.pallas_call(
        functools.partial(_ffn_kernel, final=final, n_tok_tiles=n_tok_tiles),
        out_shape=jax.ShapeDtypeStruct((rows, d), F32),
        grid=(rows // tm, f // tf),
        in_specs=in_specs,
        out_specs=pl.BlockSpec((tm, d), lambda i, j: (i, 0)),
        scratch_shapes=[pltpu.VMEM((tm, d), BF16)],
        compiler_params=pltpu.CompilerParams(
            dimension_semantics=("arbitrary", "arbitrary"), vmem_limit_bytes=VMEM_LIMIT),
        name="ffn_final" if final else "ffn",
    )(*args)


def _proj_kernel(h_ref, g_ref, w_ref, wf_ref, qkv_ref, fl_ref, xn_ref, *, n_heads, q_scale):
    hd = n_heads * HEAD_DIM
    xn_ref[...] = _rms(h_ref[...], g_ref[...]).astype(BF16)
    fl_ref[...] = _dot_nt(wf_ref[...], xn_ref[...])
    for j in range(6):
        acc = _dot(xn_ref[...], w_ref[:, j * hd:(j + 1) * hd])
        if j in (0, 3):
            acc = acc * q_scale
        for hh in range(n_heads):
            qkv_ref[j * n_heads + hh] = acc[:, hh * HEAD_DIM:(hh + 1) * HEAD_DIM].astype(BF16)


def _proj(h, g, w_qkv, wf_t, *, tm, n_heads):
    m, d = h.shape
    return pl.pallas_call(
        functools.partial(_proj_kernel, n_heads=n_heads, q_scale=LOG2E * HEAD_DIM ** -0.5),
        out_shape=(jax.ShapeDtypeStruct((6 * n_heads, m, HEAD_DIM), BF16),
                   jax.ShapeDtypeStruct((n_heads, m), F32)),
        grid=(m // tm,),
        in_specs=[
            pl.BlockSpec((tm, d), lambda i: (i, 0)),
            pl.BlockSpec((1, d), lambda i: (0, 0)),
            pl.BlockSpec(w_qkv.shape, lambda i: (0, 0), pipeline_mode=pl.Buffered(1)),
            pl.BlockSpec((n_heads, d), lambda i: (0, 0)),
        ],
        out_specs=(pl.BlockSpec((6 * n_heads, tm, HEAD_DIM), lambda i: (0, i, 0)),
                   pl.BlockSpec((n_heads, tm), lambda i: (0, i))),
        scratch_shapes=[pltpu.VMEM((tm, d), BF16)],
        compiler_params=pltpu.CompilerParams(
            dimension_semantics=("arbitrary",), vmem_limit_bytes=VMEM_LIMIT),
        name="in_proj",
    )(h, g, w_qkv, wf_t)


def _scan_kernel(fl_ref, b_ref, nc_ref, *, batch, seq):
    n_heads = fl_ref.shape[0]
    lane = lax.broadcasted_iota(jnp.int32, (n_heads, LANES), 1)
    bias = b_ref[...]

    def log_f(v):
        v = v + bias
        return jnp.minimum(v, 0.0) - jnp.log(1.0 + jnp.exp(-jnp.abs(v)))

    def scan_lanes(v):
        shift = 1
        while shift < LANES:
            v = v + jnp.where(lane >= shift, pltpu.roll(v, shift, axis=1), 0.0)
            shift *= 2
        return v

    carries = []
    for b in range(batch):
        m0 = batch * seq + b * META_BLOCK
        lf = jnp.where(lane >= META_BLOCK - N_META, log_f(fl_ref[:, m0:m0 + META_BLOCK]), 0.0)
        c_meta = scan_lanes(lf)
        nc_ref[:, m0:m0 + META_BLOCK] = -LOG2E * c_meta
        carries.append(c_meta[:, LANES - 1:LANES])

    def body(k, carries):
        out = []
        for b in range(batch):
            start = pl.multiple_of(b * seq + k * LANES, LANES)
            c = scan_lanes(log_f(fl_ref[:, pl.ds(start, LANES)])) + carries[b]
            nc_ref[:, pl.ds(start, LANES)] = -LOG2E * c
            out.append(c[:, LANES - 1:LANES])
        return tuple(out)

    lax.fori_loop(0, seq // LANES, body, tuple(carries))


def _scan(fl, b_forget, *, batch, seq):
    n_heads, m = fl.shape
    return pl.pallas_call(
        functools.partial(_scan_kernel, batch=batch, seq=seq),
        out_shape=jax.ShapeDtypeStruct((n_heads, m), F32),
        name="forget_scan",
    )(fl, b_forget.reshape(n_heads, 1))


def _fox_step(q, k, v, nc, carry, mask):
    m, l, acc = carry
    s = _dot_nt(q, k) + nc
    if mask is not None:
        s = jnp.where(mask, s, NEG_INF)
    m_new = jnp.maximum(m, jnp.max(s, axis=-1, keepdims=True))
    alpha = jnp.exp2(m - m_new)
    p = jnp.exp2(s - m_new)
    l = alpha * l + jnp.sum(p, axis=-1, keepdims=True)
    acc = alpha * acc + _dot(p.astype(BF16), v)
    return m_new, l, acc


def _sb_weights(q, k, mask, suffix):
    z = _dot_nt(q, k)
    nz = -z
    l1m = jnp.minimum(nz, 0.0) - jnp.log(1.0 + jnp.exp2(jnp.minimum(z, nz))) * LOG2E
    if mask is not None:
        l1m = jnp.where(mask, l1m, 0.0)
    incl = _dot(l1m.astype(BF16), suffix)
    e = z + incl
    if mask is not None:
        e = jnp.where(mask, e, NEG_INF)
    return jnp.exp2(e).astype(BF16), incl[:, :1]


def _sb_block(q, k, v, r, acc, mask, suffix):
    a, total = _sb_weights(q, k, mask, suffix)
    return r + total, acc + jnp.exp2(r) * _dot(a, v)


def _mixer_kernel(qf_ref, kf_ref, vf_ref, qs_ref, ks_ref, vs_ref, kfm_ref, vfm_ref, ksm_ref, vsm_ref,
                  ncx_ref, ncm_ref, gf_ref, gs_ref, of_ref, os_ref, *, tq, seq):
    n_sub = tq // SB_KEYS
    row = lax.broadcasted_iota(jnp.int32, (tq, SB_KEYS), 0)
    col = lax.broadcasted_iota(jnp.int32, (tq, SB_KEYS), 1)
    causal = col <= row
    strict = col < row
    urow = lax.broadcasted_iota(jnp.int32, (SB_KEYS, SB_KEYS), 0)
    ucol = lax.broadcasted_iota(jnp.int32, (SB_KEYS, SB_KEYS), 1)
    suffix = (urow >= ucol).astype(BF16)
    mrow = lax.broadcasted_iota(jnp.int32, (META_BLOCK, META_BLOCK), 0)
    mcol = lax.broadcasted_iota(jnp.int32, (META_BLOCK, META_BLOCK), 1)
    suffix_m = (mrow >= mcol).astype(BF16)
    meta_valid = lax.broadcasted_iota(jnp.int32, (1, META_BLOCK), 1) >= META_BLOCK - N_META
    meta_bias = jnp.where(meta_valid, ncm_ref[...], NEG_INF)

    def q_tile(t, _):
        q0 = pl.multiple_of(t * tq, tq)
        qf = qf_ref[pl.ds(q0, tq), :]
        qs = qs_ref[pl.ds(q0, tq), :]

        s = _dot_nt(qf, kfm_ref[...]) + meta_bias
        m = jnp.max(s, axis=-1, keepdims=True)
        p = jnp.exp2(s - m)
        fox = (m, jnp.sum(p, axis=-1, keepdims=True), _dot(p.astype(BF16), vfm_ref[...]))

        sb = None
        for d in reversed(range(n_sub)):
            r0 = d * SB_KEYS
            k0 = pl.multiple_of(q0 + r0, SB_KEYS)
            if sb is None:
                r, acc = jnp.zeros((tq - r0, 1), F32), jnp.zeros((tq - r0, HEAD_DIM), F32)
            else:
                r = jnp.concatenate([jnp.zeros((SB_KEYS, 1), F32), sb[0]], axis=0)
                acc = jnp.concatenate([jnp.zeros((SB_KEYS, HEAD_DIM), F32), sb[1]], axis=0)
            sb = _sb_block(qs[r0:], ks_ref[pl.ds(k0, SB_KEYS), :], vs_ref[pl.ds(k0, SB_KEYS), :],
                           r, acc, strict[:tq - r0], suffix)

        def kv_step(jj, carry):
            fox, sb = carry
            kf0 = pl.multiple_of(jj * tq, tq)
            fox = _fox_step(qf, kf_ref[pl.ds(kf0, tq), :], vf_ref[pl.ds(kf0, tq), :],
                            ncx_ref[:, pl.ds(kf0, tq)], fox, None)
            for d in reversed(range(n_sub)):
                k0 = pl.multiple_of((t - 1 - jj) * tq + d * SB_KEYS, SB_KEYS)
                sb = _sb_block(qs, ks_ref[pl.ds(k0, SB_KEYS), :], vs_ref[pl.ds(k0, SB_KEYS), :],
                               sb[0], sb[1], None, suffix)
            return fox, sb

        fox, sb = lax.fori_loop(0, t, kv_step, (fox, sb))

        for d in range(n_sub):
            r0 = d * SB_KEYS
            k0 = pl.multiple_of(q0 + r0, SB_KEYS)
            sub = _fox_step(qf[r0:], kf_ref[pl.ds(k0, SB_KEYS), :], vf_ref[pl.ds(k0, SB_KEYS), :],
                            ncx_ref[:, pl.ds(k0, SB_KEYS)], tuple(a[r0:] for a in fox), causal[:tq - r0])
            fox = tuple(jnp.concatenate([a[:r0], b], axis=0) for a, b in zip(fox, sub)) if r0 else sub
        sb = _sb_block(qs, ksm_ref[...], vsm_ref[...], sb[0], sb[1], meta_valid, suffix_m)

        of_ref[pl.ds(q0, tq), :] = _head_norm(fox[2] / fox[1], gf_ref[...]).astype(BF16)
        os_ref[pl.ds(q0, tq), :] = _head_norm(sb[1], gs_ref[...]).astype(BF16)
        return 0

    lax.fori_loop(0, seq // tq, q_tile, 0)


def _attention(qkv, nc, g_fox, g_sb, *, batch, seq, n_heads, tq):
    H = n_heads
    hd = H * HEAD_DIM
    meta0 = batch * seq // META_BLOCK
    nc3 = nc.reshape(H, 1, nc.shape[1])

    def tok(base):
        return pl.BlockSpec((None, seq, HEAD_DIM), lambda b, h: (base + h, b, 0))

    def meta(base):
        return pl.BlockSpec((None, META_BLOCK, HEAD_DIM), lambda b, h: (base + h, meta0 + b, 0))

    gain = pl.BlockSpec((1, HEAD_DIM), lambda b, h: (0, h))
    out_spec = pl.BlockSpec((seq, HEAD_DIM), lambda b, h: (b, h))
    out_shape = jax.ShapeDtypeStruct((batch * seq, hd), BF16)
    return pl.pallas_call(
        functools.partial(_mixer_kernel, tq=tq, seq=seq),
        out_shape=(out_shape, out_shape),
        grid=(batch, H),
        in_specs=[tok(0), tok(H), tok(2 * H), tok(3 * H), tok(4 * H), tok(5 * H),
                  meta(H), meta(2 * H), meta(4 * H), meta(5 * H),
                  pl.BlockSpec((None, 1, seq), lambda b, h: (h, 0, b)),
                  pl.BlockSpec((None, 1, META_BLOCK), lambda b, h: (h, 0, meta0 + b)),
                  gain, gain],
        out_specs=(out_spec, out_spec),
        compiler_params=pltpu.CompilerParams(
            dimension_semantics=("arbitrary", "arbitrary"), vmem_limit_bytes=VMEM_LIMIT),
        name="mixer_attention",
    )(qkv, qkv, qkv, qkv, qkv, qkv, qkv, qkv, qkv, qkv, nc3, nc3, g_fox, g_sb)


def _meta_attn_kernel(qkv_ref, ncm_ref, g_ref, o_ref, *, n_heads):
    H = n_heads
    row = lax.broadcasted_iota(jnp.int32, (META_BLOCK, META_BLOCK), 0)
    col = lax.broadcasted_iota(jnp.int32, (META_BLOCK, META_BLOCK), 1)
    valid = col >= META_BLOCK - N_META
    fox_allowed = (col <= row) & (valid | (col == row))
    sb_allowed = (col < row) & valid
    suffix = (row >= col).astype(BF16)
    for hh in range(H):
        s = jnp.where(fox_allowed, _dot_nt(qkv_ref[hh], qkv_ref[H + hh]) + ncm_ref[hh], NEG_INF)
        p = jnp.exp2(s - jnp.max(s, axis=-1, keepdims=True))
        o = _dot(p.astype(BF16), qkv_ref[2 * H + hh]) / jnp.sum(p, axis=-1, keepdims=True)
        cols = slice(hh * HEAD_DIM, (hh + 1) * HEAD_DIM)
        o_ref[:, cols] = _head_norm(o, g_ref[:, cols]).astype(BF16)
    for hh in range(H):
        a, _ = _sb_weights(qkv_ref[3 * H + hh], qkv_ref[4 * H + hh], sb_allowed, suffix)
        cols = slice((H + hh) * HEAD_DIM, (H + hh + 1) * HEAD_DIM)
        o_ref[:, cols] = _head_norm(_dot(a, qkv_ref[5 * H + hh]), g_ref[:, cols]).astype(BF16)


def _meta_attention(qkv, nc, g_cat, *, batch, seq, n_heads):
    H = n_heads
    meta0 = batch * seq // META_BLOCK
    nc3 = nc.reshape(H, 1, nc.shape[1])
    return pl.pallas_call(
        functools.partial(_meta_attn_kernel, n_heads=H),
        out_shape=jax.ShapeDtypeStruct((batch * META_BLOCK, 2 * H * HEAD_DIM), BF16),
        grid=(batch,),
        in_specs=[pl.BlockSpec((6 * H, META_BLOCK, HEAD_DIM), lambda b: (0, meta0 + b, 0)),
                  pl.BlockSpec((H, 1, META_BLOCK), lambda b: (0, 0, meta0 + b)),
                  pl.BlockSpec((1, 2 * H * HEAD_DIM), lambda b: (0, 0))],
        out_specs=pl.BlockSpec((META_BLOCK, 2 * H * HEAD_DIM), lambda b: (b, 0)),
        compiler_params=pltpu.CompilerParams(dimension_semantics=("arbitrary",)),
        name="meta_attention",
    )(qkv, nc3, g_cat)


def _out_proj_kernel(h_ref, of_ref, os_ref, om_ref, w_ref, o_ref, *, n_tok_tiles, hd):
    i = pl.program_id(0)

    @pl.when(i < n_tok_tiles)
    def _():
        o_ref[...] = h_ref[...] + _dot(of_ref[...], w_ref[:hd, :]) + _dot(os_ref[...], w_ref[hd:, :])

    @pl.when(i >= n_tok_tiles)
    def _():
        o_ref[...] = h_ref[...] + _dot(om_ref[...], w_ref[...])


def _out_proj(h, o_f, o_s, o_m, w, *, tm):
    m, d = h.shape
    hd = o_f.shape[1]
    n_tok_tiles = o_f.shape[0] // tm
    assert o_m.shape[0] == tm and m == (n_tok_tiles + 1) * tm
    last = n_tok_tiles - 1
    return pl.pallas_call(
        functools.partial(_out_proj_kernel, n_tok_tiles=n_tok_tiles, hd=hd),
        out_shape=jax.ShapeDtypeStruct((m, d), F32),
        grid=(m // tm,),
        in_specs=[
            pl.BlockSpec((tm, d), lambda i: (i, 0)),
            pl.BlockSpec((tm, hd), lambda i: (jnp.minimum(i, last), 0)),
            pl.BlockSpec((tm, hd), lambda i: (jnp.minimum(i, last), 0)),
            pl.BlockSpec((tm, 2 * hd), lambda i: (0, 0)),
            pl.BlockSpec((2 * hd, d), lambda i: (0, 0)),
        ],
        out_specs=pl.BlockSpec((tm, d), lambda i: (i, 0)),
        compiler_params=pltpu.CompilerParams(
            dimension_semantics=("arbitrary",), vmem_limit_bytes=VMEM_LIMIT),
        name="out_proj",
    )(h, o_f, o_s, o_m, w)


def kernel(x, meta_tokens, ffn1_norm, ffn1_w_gate, ffn1_w_up, ffn1_w_down, mix_norm, w_in, b_forget, g_fox, g_sb, w_out, ffn2_norm, ffn2_w_gate, ffn2_w_up, ffn2_w_down, final_norm):
    batch, seq, d = x.shape
    depth = w_in.shape[0]
    n_heads = g_fox.shape[1] // HEAD_DIM
    hd = n_heads * HEAD_DIM
    f = ffn1_w_gate.shape[2]
    assert meta_tokens.shape[0] == N_META and w_in.shape[2] == 6 * hd + n_heads

    tm = batch * META_BLOCK
    assert (batch * seq) % tm == 0
    tf = 512 if f % 512 == 0 else LANES
    tq = 1024
    assert seq % tq == 0

    lead = jnp.concatenate([jnp.zeros((META_BLOCK - N_META, d), x.dtype), meta_tokens.astype(x.dtype)], axis=0)
    lead = jnp.tile(lead, (batch, 1))
    h = x.reshape(batch * seq, d)

    row = lambda v: v.reshape(1, -1)
    for l in range(depth):
        h = _ffn(h, row(ffn1_norm[l]), ffn1_w_gate[l].astype(BF16), ffn1_w_up[l].astype(BF16),
                 ffn1_w_down[l].astype(BF16), tm=tm, tf=tf, lead=lead if l == 0 else None)

        qkv, fl = _proj(h, row(mix_norm[l]), w_in[l, :, :6 * hd].astype(BF16),
                        w_in[l, :, 6 * hd:].T.astype(BF16), tm=tm, n_heads=n_heads)
        nc = _scan(fl, b_forget[l], batch=batch, seq=seq)
        o_f, o_s = _attention(qkv, nc, row(g_fox[l]), row(g_sb[l]),
                              batch=batch, seq=seq, n_heads=n_heads, tq=tq)
        o_m = _meta_attention(qkv, nc, row(jnp.concatenate([g_fox[l], g_sb[l]])),
                              batch=batch, seq=seq, n_heads=n_heads)
        h = _out_proj(h, o_f, o_s, o_m, w_out[l].astype(BF16), tm=tm)

        last = l == depth - 1
        h = _ffn(h, row(ffn2_norm[l]), ffn2_w_gate[l].astype(BF16), ffn2_w_up[l].astype(BF16),
                 ffn2_w_down[l].astype(BF16), tm=tm, tf=tf,
                 final_gain=row(final_norm) if last else None, rows=batch * seq if last else None)
    return h.reshape(batch, seq, d)
```

```python
import functools

import jax
import jax.numpy as jnp
from jax import lax
from jax.experimental import pallas as pl
from jax.experimental.pallas import tpu as pltpu

N_META = 16
HEAD_DIM = 128
META_BLOCK = 128
EPS = 1e-6
LANES = 128
SB_KEYS = 256
LOG2E = 1.4426950408889634
FOX_DIAG_KEYS = 512
CAST_BLOCK_BYTES = 4 * 1024 * 1024
VMEM_LIMIT = 56 * 1024 * 1024

F32 = jnp.float32
BF16 = jnp.bfloat16
NEG_INF = float("-inf")


def _dot(a, b):
    return jnp.dot(a, b, preferred_element_type=F32)


def _dot_nt(a, b):
    return lax.dot_general(a, b, (((1,), (1,)), ((), ())), preferred_element_type=F32)


def _rms(x, g):
    return x * lax.rsqrt(jnp.mean(x * x, axis=-1, keepdims=True) + EPS) * g


def _head_norm(o, g):
    return o * lax.rsqrt(jnp.mean(o * o, axis=-1, keepdims=True) + EPS) * g


def _cast_kernel(x_ref, *o_refs, split):
    x = x_ref[...]
    if split is None:
        o_refs[0][...] = x.astype(BF16)
    else:
        o_refs[0][...] = x[:, :split].astype(BF16)
        o_refs[1][...] = x[:, split:]


def _to_bf16(w, split=None):
    depth, r, c = w.shape
    rows = depth * r
    br = max(8, min(rows, CAST_BLOCK_BYTES // (4 * c) // 8 * 8))
    while rows % br:
        br -= 8
    if split is None:
        out_shape = jax.ShapeDtypeStruct((rows, c), BF16)
        out_specs = pl.BlockSpec((br, c), lambda i: (i, 0))
    else:
        out_shape = (jax.ShapeDtypeStruct((rows, split), BF16), jax.ShapeDtypeStruct((rows, c - split), w.dtype))
        out_specs = (pl.BlockSpec((br, split), lambda i: (i, 0)), pl.BlockSpec((br, c - split), lambda i: (i, 0)))
    out = pl.pallas_call(
        functools.partial(_cast_kernel, split=split),
        out_shape=out_shape,
        grid=(rows // br,),
        in_specs=[pl.BlockSpec((br, c), lambda i: (i, 0))],
        out_specs=out_specs,
        compiler_params=pltpu.CompilerParams(dimension_semantics=("arbitrary",), vmem_limit_bytes=VMEM_LIMIT),
        name="cast_bf16",
    )(w.reshape(rows, c))
    if split is None:
        return out.reshape(depth, r, c)
    return out[0].reshape(depth, r, split), out[1].reshape(depth, r, c - split)


def _ffn_kernel(h_ref, *rest, final, n_tok_tiles):
    lead_ref = None
    if n_tok_tiles is not None:
        lead_ref, *rest = rest
    g_ref, wg_ref, wu_ref, wd_ref, *rest = rest
    if final:
        gf_ref, o_ref, xn_ref = rest
    else:
        o_ref, xn_ref = rest
    i = pl.program_id(0)
    j = pl.program_id(1)

    def start_tile(src_ref):
        h = src_ref[...]
        xn_ref[...] = _rms(h, g_ref[...]).astype(BF16)
        o_ref[...] = h

    if lead_ref is None:
        pl.when(j == 0)(lambda: start_tile(h_ref))
    else:
        pl.when((j == 0) & (i < n_tok_tiles))(lambda: start_tile(h_ref))
        pl.when((j == 0) & (i >= n_tok_tiles))(lambda: start_tile(lead_ref))

    xn = xn_ref[...]
    gate = _dot(xn, wg_ref[...])
    up = _dot(xn, wu_ref[...])
    a = (gate * jax.nn.sigmoid(gate) * up * 0.5).astype(BF16)
    o_ref[...] += _dot(a, wd_ref[...])

    if final:
        @pl.when(j == pl.num_programs(1) - 1)
        def _():
            o_ref[...] = _rms(o_ref[...], gf_ref[...])


def _ffn(h, g, wg, wu, wd, *, layer, tm, tf, final_gain=None, rows=None, lead=None):
    m, d = h.shape
    f = wg.shape[2]
    final = final_gain is not None
    n_tok_tiles = None
    if lead is None:
        rows = m if rows is None else rows
        in_specs = [pl.BlockSpec((tm, d), lambda i, j: (i, 0))]
        args = [h]
    else:
        assert lead.shape == (tm, d) and m % tm == 0 and rows is None
        n_tok_tiles = m // tm
        rows = m + tm
        in_specs = [pl.BlockSpec((tm, d), lambda i, j: (jnp.minimum(i, n_tok_tiles - 1), 0)),
                    pl.BlockSpec((tm, d), lambda i, j: (0, 0))]
        args = [h, lead]
    in_specs += [
        pl.BlockSpec((1, d), lambda i, j: (0, 0)),
        pl.BlockSpec((None, d, tf), lambda i, j: (layer, 0, j)),
        pl.BlockSpec((None, d, tf), lambda i, j: (layer, 0, j)),
        pl.BlockSpec((None, tf, d), lambda i, j: (layer, j, 0)),
    ]
    args += [g, wg, wu, wd]
    if final:
        in_specs.append(pl.BlockSpec((1, d), lambda i, j: (0, 0)))
        args.append(final_gain)
    return pl.pallas_call(
        functools.partial(_ffn_kernel, final=final, n_tok_tiles=n_tok_tiles),
        out_shape=jax.ShapeDtypeStruct((rows, d), F32),
        grid=(rows // tm, f // tf),
        in_specs=in_specs,
        out_specs=pl.BlockSpec((tm, d), lambda i, j: (i, 0)),
        scratch_shapes=[pltpu.VMEM((tm, d), BF16)],
        compiler_params=pltpu.CompilerParams(
            dimension_semantics=("arbitrary", "arbitrary"), vmem_limit_bytes=VMEM_LIMIT),
        name="ffn_final" if final else "ffn",
    )(*args)


def _proj_kernel(h_ref, g_ref, w_ref, wf_ref, qkv_ref, fl_ref, xn_ref, *, n_heads, q_scale):
    hd = n_heads * HEAD_DIM
    xn_ref[...] = _rms(h_ref[...], g_ref[...]).astype(BF16)
    fl_ref[...] = _dot_nt(wf_ref[...], xn_ref[...])
    for j in range(6):
        acc = _dot(xn_ref[...], w_ref[:, j * hd:(j + 1) * hd])
        if j in (0, 3):
            acc = acc * q_scale
        for hh in range(n_heads):
            qkv_ref[j * n_heads + hh] = acc[:, hh * HEAD_DIM:(hh + 1) * HEAD_DIM].astype(BF16)


def _proj(h, g, w_qkv, wf_t, *, layer, tm, n_heads):
    m, d = h.shape
    return pl.pallas_call(
        functools.partial(_proj_kernel, n_heads=n_heads, q_scale=LOG2E * HEAD_DIM ** -0.5),
        out_shape=(jax.ShapeDtypeStruct((6 * n_heads, m, HEAD_DIM), BF16),
                   jax.ShapeDtypeStruct((n_heads, m), F32)),
        grid=(m // tm,),
        in_specs=[
            pl.BlockSpec((tm, d), lambda i: (i, 0)),
            pl.BlockSpec((1, d), lambda i: (0, 0)),
            pl.BlockSpec((None,) + w_qkv.shape[1:], lambda i: (layer, 0, 0), pipeline_mode=pl.Buffered(1)),
            pl.BlockSpec((n_heads, d), lambda i: (0, 0)),
        ],
        out_specs=(pl.BlockSpec((6 * n_heads, tm, HEAD_DIM), lambda i: (0, i, 0)),
                   pl.BlockSpec((n_heads, tm), lambda i: (0, i))),
        scratch_shapes=[pltpu.VMEM((tm, d), BF16)],
        compiler_params=pltpu.CompilerParams(
            dimension_semantics=("arbitrary",), vmem_limit_bytes=VMEM_LIMIT),
        name="in_proj",
    )(h, g, w_qkv, wf_t)


def _scan_kernel(fl_ref, b_ref, nc_ref, *, batch, seq):
    n_heads = fl_ref.shape[0]
    lane = lax.broadcasted_iota(jnp.int32, (n_heads, LANES), 1)
    bias = b_ref[...]

    def log_f(v):
        v = v + bias
        return jnp.minimum(v, 0.0) - jnp.log(1.0 + jnp.exp(-jnp.abs(v)))

    def scan_lanes(v):
        shift = 1
        while shift < LANES:
            v = v + jnp.where(lane >= shift, pltpu.roll(v, shift, axis=1), 0.0)
            shift *= 2
        return v

    carries = []
    for b in range(batch):
        m0 = batch * seq + b * META_BLOCK
        lf = jnp.where(lane >= META_BLOCK - N_META, log_f(fl_ref[:, m0:m0 + META_BLOCK]), 0.0)
        c_meta = scan_lanes(lf)
        nc_ref[:, m0:m0 + META_BLOCK] = -LOG2E * c_meta
        carries.append(c_meta[:, LANES - 1:LANES])

    def body(k, carries):
        out = []
        for b in range(batch):
            start = pl.multiple_of(b * seq + k * LANES, LANES)
            c = scan_lanes(log_f(fl_ref[:, pl.ds(start, LANES)])) + carries[b]
            nc_ref[:, pl.ds(start, LANES)] = -LOG2E * c
            out.append(c[:, LANES - 1:LANES])
        return tuple(out)

    lax.fori_loop(0, seq // LANES, body, tuple(carries))


def _scan(fl, b_forget, *, batch, seq):
    n_heads, m = fl.shape
    return pl.pallas_call(
        functools.partial(_scan_kernel, batch=batch, seq=seq),
        out_shape=jax.ShapeDtypeStruct((n_heads, m), F32),
        name="forget_scan",
    )(fl, b_forget.reshape(n_heads, 1))


def _fox_step(q, k, v, nc, carry, mask):
    m, l, acc = carry
    s = _dot_nt(q, k) + nc
    if mask is not None:
        s = jnp.where(mask, s, NEG_INF)
    m_new = jnp.maximum(m, jnp.max(s, axis=-1, keepdims=True))
    alpha = jnp.exp2(m - m_new)
    p = jnp.exp2(s - m_new)
    l = alpha * l + jnp.sum(p, axis=-1, keepdims=True)
    acc = alpha * acc + _dot(p.astype(BF16), v)
    return m_new, l, acc


def _sb_weights(q, k, mask, suffix):
    z = _dot_nt(q, k)
    nz = -z
    l1m = jnp.minimum(nz, 0.0) - jnp.log(1.0 + jnp.exp2(jnp.minimum(z, nz))) * LOG2E
    if mask is not None:
        l1m = jnp.where(mask, l1m, 0.0)
    incl = _dot(l1m.astype(BF16), suffix)
    e = z + incl
    if mask is not None:
        e = jnp.where(mask, e, NEG_INF)
    return jnp.exp2(e).astype(BF16), incl[:, :1]


def _sb_block(q, k, v, r, acc, mask, suffix):
    a, total = _sb_weights(q, k, mask, suffix)
    return r + total, acc + jnp.exp2(r) * _dot(a, v)


def _mixer_kernel(qf_ref, kf_ref, vf_ref, qs_ref, ks_ref, vs_ref, kfm_ref, vfm_ref, ksm_ref, vsm_ref,
                  ncx_ref, ncm_ref, gf_ref, gs_ref, of_ref, os_ref, *, tq, seq):
    n_sub = tq // SB_KEYS
    row = lax.broadcasted_iota(jnp.int32, (tq, SB_KEYS), 0)
    col = lax.broadcasted_iota(jnp.int32, (tq, SB_KEYS), 1)
    strict = col < row
    causal = (lax.broadcasted_iota(jnp.int32, (tq, FOX_DIAG_KEYS), 1)
              <= lax.broadcasted_iota(jnp.int32, (tq, FOX_DIAG_KEYS), 0))
    urow = lax.broadcasted_iota(jnp.int32, (SB_KEYS, SB_KEYS), 0)
    ucol = lax.broadcasted_iota(jnp.int32, (SB_KEYS, SB_KEYS), 1)
    suffix = (urow >= ucol).astype(BF16)
    mrow = lax.broadcasted_iota(jnp.int32, (META_BLOCK, META_BLOCK), 0)
    mcol = lax.broadcasted_iota(jnp.int32, (META_BLOCK, META_BLOCK), 1)
    suffix_m = (mrow >= mcol).astype(BF16)
    meta_valid = lax.broadcasted_iota(jnp.int32, (1, META_BLOCK), 1) >= META_BLOCK - N_META
    meta_bias = jnp.where(meta_valid, ncm_ref[...], NEG_INF)

    def q_tile(t, _):
        q0 = pl.multiple_of(t * tq, tq)
        qf = qf_ref[pl.ds(q0, tq), :]
        qs = qs_ref[pl.ds(q0, tq), :]

        s = _dot_nt(qf, kfm_ref[...]) + meta_bias
        m = jnp.max(s, axis=-1, keepdims=True)
        p = jnp.exp2(s - m)
        fox = (m, jnp.sum(p, axis=-1, keepdims=True), _dot(p.astype(BF16), vfm_ref[...]))

        sb = None
        for d in reversed(range(n_sub)):
            r0 = d * SB_KEYS
            k0 = pl.multiple_of(q0 + r0, SB_KEYS)
            if sb is None:
                r, acc = jnp.zeros((tq - r0, 1), F32), jnp.zeros((tq - r0, HEAD_DIM), F32)
            else:
                r = jnp.concatenate([jnp.zeros((SB_KEYS, 1), F32), sb[0]], axis=0)
                acc = jnp.concatenate([jnp.zeros((SB_KEYS, HEAD_DIM), F32), sb[1]], axis=0)
            sb = _sb_block(qs[r0:], ks_ref[pl.ds(k0, SB_KEYS), :], vs_ref[pl.ds(k0, SB_KEYS), :],
                           r, acc, strict[:tq - r0], suffix)

        def kv_step(jj, carry):
            fox, sb = carry
            kf0 = pl.multiple_of(jj * tq, tq)
            fox = _fox_step(qf, kf_ref[pl.ds(kf0, tq), :], vf_ref[pl.ds(kf0, tq), :],
                            ncx_ref[:, pl.ds(kf0, tq)], fox, None)
            for d in reversed(range(n_sub)):
                k0 = pl.multiple_of((t - 1 - jj) * tq + d * SB_KEYS, SB_KEYS)
                sb = _sb_block(qs, ks_ref[pl.ds(k0, SB_KEYS), :], vs_ref[pl.ds(k0, SB_KEYS), :],
                               sb[0], sb[1], None, suffix)
            return fox, sb

        fox, sb = lax.fori_loop(0, t, kv_step, (fox, sb))

        for d in range(tq // FOX_DIAG_KEYS):
            r0 = d * FOX_DIAG_KEYS
            k0 = pl.multiple_of(q0 + r0, FOX_DIAG_KEYS)
            sub = _fox_step(qf[r0:], kf_ref[pl.ds(k0, FOX_DIAG_KEYS), :], vf_ref[pl.ds(k0, FOX_DIAG_KEYS), :],
                            ncx_ref[:, pl.ds(k0, FOX_DIAG_KEYS)], tuple(a[r0:] for a in fox),
                            causal[:tq - r0])
            fox = tuple(jnp.concatenate([a[:r0], b], axis=0) for a, b in zip(fox, sub)) if r0 else sub
        sb = _sb_block(qs, ksm_ref[...], vsm_ref[...], sb[0], sb[1], meta_valid, suffix_m)

        of_ref[pl.ds(q0, tq), :] = _head_norm(fox[2] / fox[1], gf_ref[...]).astype(BF16)
        os_ref[pl.ds(q0, tq), :] = _head_norm(sb[1], gs_ref[...]).astype(BF16)
        return 0

    lax.fori_loop(0, seq // tq, q_tile, 0)


def _attention(qkv, nc, g_fox, g_sb, *, batch, seq, n_heads, tq):
    H = n_heads
    hd = H * HEAD_DIM
    meta0 = batch * seq // META_BLOCK
    nc3 = nc.reshape(H, 1, nc.shape[1])

    def tok(base):
        return pl.BlockSpec((None, seq, HEAD_DIM), lambda b, h: (base + h, b, 0))

    def meta(base):
        return pl.BlockSpec((None, META_BLOCK, HEAD_DIM), lambda b, h: (base + h, meta0 + b, 0))

    gain = pl.BlockSpec((1, HEAD_DIM), lambda b, h: (0, h))
    out_spec = pl.BlockSpec((seq, HEAD_DIM), lambda b, h: (b, h))
    out_shape = jax.ShapeDtypeStruct((batch * seq, hd), BF16)
    return pl.pallas_call(
        functools.partial(_mixer_kernel, tq=tq, seq=seq),
        out_shape=(out_shape, out_shape),
        grid=(batch, H),
        in_specs=[tok(0), tok(H), tok(2 * H), tok(3 * H), tok(4 * H), tok(5 * H),
                  meta(H), meta(2 * H), meta(4 * H), meta(5 * H),
                  pl.BlockSpec((None, 1, seq), lambda b, h: (h, 0, b)),
                  pl.BlockSpec((None, 1, META_BLOCK), lambda b, h: (h, 0, meta0 + b)),
                  gain, gain],
        out_specs=(out_spec, out_spec),
        compiler_params=pltpu.CompilerParams(
            dimension_semantics=("arbitrary", "arbitrary"), vmem_limit_bytes=VMEM_LIMIT),
        name="mixer_attention",
    )(qkv, qkv, qkv, qkv, qkv, qkv, qkv, qkv, qkv, qkv, nc3, nc3, g_fox, g_sb)


def _meta_attn_kernel(qkv_ref, ncm_ref, g_ref, o_ref, *, n_heads):
    H = n_heads
    row = lax.broadcasted_iota(jnp.int32, (META_BLOCK, META_BLOCK), 0)
    col = lax.broadcasted_iota(jnp.int32, (META_BLOCK, META_BLOCK), 1)
    valid = col >= META_BLOCK - N_META
    fox_allowed = (col <= row) & (valid | (col == row))
    sb_allowed = (col < row) & valid
    suffix = (row >= col).astype(BF16)
    for hh in range(H):
        s = jnp.where(fox_allowed, _dot_nt(qkv_ref[hh], qkv_ref[H + hh]) + ncm_ref[hh], NEG_INF)
        p = jnp.exp2(s - jnp.max(s, axis=-1, keepdims=True))
        o = _dot(p.astype(BF16), qkv_ref[2 * H + hh]) / jnp.sum(p, axis=-1, keepdims=True)
        cols = slice(hh * HEAD_DIM, (hh + 1) * HEAD_DIM)
        o_ref[:, cols] = _head_norm(o, g_ref[:, cols]).astype(BF16)
    for hh in range(H):
        a, _ = _sb_weights(qkv_ref[3 * H + hh], qkv_ref[4 * H + hh], sb_allowed, suffix)
        cols = slice((H + hh) * HEAD_DIM, (H + hh + 1) * HEAD_DIM)
        o_ref[:, cols] = _head_norm(_dot(a, qkv_ref[5 * H + hh]), g_ref[:, cols]).astype(BF16)


def _meta_attention(qkv, nc, g_cat, *, batch, seq, n_heads):
    H = n_heads
    meta0 = batch * seq // META_BLOCK
    nc3 = nc.reshape(H, 1, nc.shape[1])
    return pl.pallas_call(
        functools.partial(_meta_attn_kernel, n_heads=H),
        out_shape=jax.ShapeDtypeStruct((batch * META_BLOCK, 2 * H * HEAD_DIM), BF16),
        grid=(batch,),
        in_specs=[pl.BlockSpec((6 * H, META_BLOCK, HEAD_DIM), lambda b: (0, meta0 + b, 0)),
                  pl.BlockSpec((H, 1, META_BLOCK), lambda b: (0, 0, meta0 + b)),
                  pl.BlockSpec((1, 2 * H * HEAD_DIM), lambda b: (0, 0))],
        out_specs=pl.BlockSpec((META_BLOCK, 2 * H * HEAD_DIM), lambda b: (b, 0)),
        compiler_params=pltpu.CompilerParams(dimension_semantics=("arbitrary",)),
        name="meta_attention",
    )(qkv, nc3, g_cat)


def _out_proj_kernel(h_ref, of_ref, os_ref, om_ref, w_ref, o_ref, *, n_tok_tiles, hd):
    i = pl.program_id(0)

    @pl.when(i < n_tok_tiles)
    def _():
        o_ref[...] = h_ref[...] + _dot(of_ref[...], w_ref[:hd, :]) + _dot(os_ref[...], w_ref[hd:, :])

    @pl.when(i >= n_tok_tiles)
    def _():
        o_ref[...] = h_ref[...] + _dot(om_ref[...], w_ref[...])


def _out_proj(h, o_f, o_s, o_m, w, *, layer, tm):
    m, d = h.shape
    hd = o_f.shape[1]
    n_tok_tiles = o_f.shape[0] // tm
    assert o_m.shape[0] == tm and m == (n_tok_tiles + 1) * tm
    last = n_tok_tiles - 1
    return pl.pallas_call(
        functools.partial(_out_proj_kernel, n_tok_tiles=n_tok_tiles, hd=hd),
        out_shape=jax.ShapeDtypeStruct((m, d), F32),
        grid=(m // tm,),
        in_specs=[
            pl.BlockSpec((tm, d), lambda i: (i, 0)),
            pl.BlockSpec((tm, hd), lambda i: (jnp.minimum(i, last), 0)),
            pl.BlockSpec((tm, hd), lambda i: (jnp.minimum(i, last), 0)),
            pl.BlockSpec((tm, 2 * hd), lambda i: (0, 0)),
            pl.BlockSpec((None, 2 * hd, d), lambda i: (layer, 0, 0)),
        ],
        out_specs=pl.BlockSpec((tm, d), lambda i: (i, 0)),
        compiler_params=pltpu.CompilerParams(
            dimension_semantics=("arbitrary",), vmem_limit_bytes=VMEM_LIMIT),
        name="out_proj",
    )(h, o_f, o_s, o_m, w)


def kernel(x, meta_tokens, ffn1_norm, ffn1_w_gate, ffn1_w_up, ffn1_w_down, mix_norm, w_in, b_forget, g_fox, g_sb, w_out, ffn2_norm, ffn2_w_gate, ffn2_w_up, ffn2_w_down, final_norm):
    batch, seq, d = x.shape
    depth = w_in.shape[0]
    n_heads = g_fox.shape[1] // HEAD_DIM
    hd = n_heads * HEAD_DIM
    f = ffn1_w_gate.shape[2]
    assert meta_tokens.shape[0] == N_META and w_in.shape[2] == 6 * hd + n_heads

    tm = batch * META_BLOCK
    assert (batch * seq) % tm == 0
    tf = 512 if f % 512 == 0 else LANES
    tq = 1024
    assert seq % tq == 0

    lead = jnp.concatenate([jnp.zeros((META_BLOCK - N_META, d), x.dtype), meta_tokens.astype(x.dtype)], axis=0)
    lead = jnp.tile(lead, (batch, 1))
    h = x.reshape(batch * seq, d)

    row = lambda v: v.reshape(1, -1)
    w1g, w1u, w1d = _to_bf16(ffn1_w_gate), _to_bf16(ffn1_w_up), _to_bf16(ffn1_w_down)
    w2g, w2u, w2d = _to_bf16(ffn2_w_gate), _to_bf16(ffn2_w_up), _to_bf16(ffn2_w_down)
    w_qkv, w_f = _to_bf16(w_in, split=6 * hd)
    w_f_t = jnp.swapaxes(w_f, 1, 2).astype(BF16)
    w_o = _to_bf16(w_out)
    for l in range(depth):
        h = _ffn(h, row(ffn1_norm[l]), w1g, w1u, w1d, layer=l, tm=tm, tf=tf, lead=lead if l == 0 else None)

        qkv, fl = _proj(h, row(mix_norm[l]), w_qkv, w_f_t[l], layer=l, tm=tm, n_heads=n_heads)
        nc = _scan(fl, b_forget[l], batch=batch, seq=seq)
        o_f, o_s = _attention(qkv, nc, row(g_fox[l]), row(g_sb[l]),
                              batch=batch, seq=seq, n_heads=n_heads, tq=tq)
        o_m = _meta_attention(qkv, nc, row(jnp.concatenate([g_fox[l], g_sb[l]])),
                              batch=batch, seq=seq, n_heads=n_heads)
        h = _out_proj(h, o_f, o_s, o_m, w_o, layer=l, tm=tm)

        last = l == depth - 1
        h = _ffn(h, row(ffn2_norm[l]), w2g, w2u, w2d, layer=l, tm=tm, tf=tf,
                 final_gain=row(final_norm) if last else None, rows=batch * seq if last else None)
    return h.reshape(batch, seq, d)
```

```python
import functools

import jax
import jax.numpy as jnp
from jax import lax
from jax.experimental import pallas as pl
from jax.experimental.pallas import tpu as pltpu

N_META = 16
HEAD_DIM = 128
META_BLOCK = 128
EPS = 1e-6
LANES = 128
SB_KEYS = 256
FOX_DIAG_KEYS = 512
LOG2E = 1.4426950408889634
FFN_ROWS = 1024
PROJ_ROWS = 512
ATTN_ROWS = 1024
CAST_BLOCK_BYTES = 4 * 1024 * 1024
VMEM_LIMIT = 60 * 1024 * 1024

F32 = jnp.float32
BF16 = jnp.bfloat16
NEG_INF = float("-inf")


def _dot(a, b):
    return jnp.dot(a, b, preferred_element_type=F32)


def _dot_nt(a, b):
    return lax.dot_general(a, b, (((1,), (1,)), ((), ())), preferred_element_type=F32)


def _rms(x, g):
    return x * lax.rsqrt(jnp.mean(x * x, axis=-1, keepdims=True) + EPS) * g


def _head_norm(o, g):
    return o * lax.rsqrt(jnp.mean(o * o, axis=-1, keepdims=True) + EPS) * g


def _row_tile(rows, want):
    tile = min(rows, want)
    assert rows % tile == 0
    return tile


def _cast_kernel(x_ref, o_ref):
    o_ref[...] = x_ref[...].astype(BF16)


def _largest_divisor(n, step, limit):
    best = None
    for cand in range(step, min(n, limit) + 1, step):
        if n % cand == 0:
            best = cand
    return best


def _to_bf16(w):
    depth, r, c = w.shape
    br = _largest_divisor(r, 8, CAST_BLOCK_BYTES // (4 * c))
    if br is not None and 4 * br * c * 4 >= CAST_BLOCK_BYTES:
        block = (None, br, c)
    else:
        block = (None, r, _largest_divisor(c, LANES, max(LANES, CAST_BLOCK_BYTES // (4 * r))))
    spec = pl.BlockSpec(block, lambda l, i, j: (l, i, j))
    return pl.pallas_call(
        _cast_kernel,
        out_shape=jax.ShapeDtypeStruct(w.shape, BF16),
        grid=(depth, r // block[1], c // block[2]),
        in_specs=[spec],
        out_specs=spec,
        compiler_params=pltpu.CompilerParams(
            dimension_semantics=("arbitrary",) * 3, vmem_limit_bytes=VMEM_LIMIT),
        name="cast_bf16",
    )(w)


def _ffn_kernel(h_ref, g_ref, wg_ref, wu_ref, wd_ref, *rest, final):
    if final:
        gf_ref, o_ref, xn_ref = rest
    else:
        o_ref, xn_ref = rest
    j = pl.program_id(1)

    @pl.when(j == 0)
    def _():
        h = h_ref[...]
        xn_ref[...] = _rms(h, g_ref[...]).astype(BF16)
        o_ref[...] = h

    xn = xn_ref[...]
    gate = _dot(xn, wg_ref[...])
    up = _dot(xn, wu_ref[...])
    a = (gate * jax.nn.sigmoid(gate) * up * 0.5).astype(BF16)
    o_ref[...] += _dot(a, wd_ref[...])

    if final:
        @pl.when(j == pl.num_programs(1) - 1)
        def _():
            o_ref[...] = _rms(o_ref[...], gf_ref[...])


def _ffn(h, g, wg, wu, wd, *, layer, tf, final_gain=None):
    rows, d = h.shape
    f = wg.shape[2]
    tm = _row_tile(rows, FFN_ROWS)
    final = final_gain is not None
    in_specs = [
        pl.BlockSpec((tm, d), lambda i, j: (i, 0)),
        pl.BlockSpec((1, d), lambda i, j: (0, 0)),
        pl.BlockSpec((None, d, tf), lambda i, j: (layer, 0, j)),
        pl.BlockSpec((None, d, tf), lambda i, j: (layer, 0, j)),
        pl.BlockSpec((None, tf, d), lambda i, j: (layer, j, 0)),
    ]
    args = [h, g, wg, wu, wd]
    if final:
        in_specs.append(pl.BlockSpec((1, d), lambda i, j: (0, 0)))
        args.append(final_gain)
    return pl.pallas_call(
        functools.partial(_ffn_kernel, final=final),
        out_shape=jax.ShapeDtypeStruct((rows, d), F32),
        grid=(rows // tm, f // tf),
        in_specs=in_specs,
        out_specs=pl.BlockSpec((tm, d), lambda i, j: (i, 0)),
        scratch_shapes=[pltpu.VMEM((tm, d), BF16)],
        compiler_params=pltpu.CompilerParams(
            dimension_semantics=("arbitrary", "arbitrary"), vmem_limit_bytes=VMEM_LIMIT),
        name="ffn_final" if final else "ffn",
    )(*args)


def _proj_kernel(h_ref, g_ref, w_ref, qkv_ref, fl_ref, xn_ref, *, n_heads, q_scale):
    hd = n_heads * HEAD_DIM
    xn_ref[...] = _rms(h_ref[...], g_ref[...]).astype(BF16)
    fl_ref[...] = _dot_nt(w_ref[6 * hd:, :], xn_ref[...])
    for j in range(6):
        acc = _dot_nt(xn_ref[...], w_ref[j * hd:(j + 1) * hd, :])
        if j in (0, 3):
            acc = acc * q_scale
        for hh in range(n_heads):
            qkv_ref[j * n_heads + hh] = acc[:, hh * HEAD_DIM:(hh + 1) * HEAD_DIM].astype(BF16)


def _proj(h, g, w_t, *, layer, n_heads):
    rows, d = h.shape
    tm = _row_tile(rows, PROJ_ROWS)
    return pl.pallas_call(
        functools.partial(_proj_kernel, n_heads=n_heads, q_scale=LOG2E * HEAD_DIM ** -0.5),
        out_shape=(jax.ShapeDtypeStruct((6 * n_heads, rows, HEAD_DIM), BF16),
                   jax.ShapeDtypeStruct((n_heads, rows), F32)),
        grid=(rows // tm,),
        in_specs=[
            pl.BlockSpec((tm, d), lambda i: (i, 0)),
            pl.BlockSpec((1, d), lambda i: (0, 0)),
            pl.BlockSpec((None,) + w_t.shape[1:], lambda i: (layer, 0, 0), pipeline_mode=pl.Buffered(1)),
        ],
        out_specs=(pl.BlockSpec((6 * n_heads, tm, HEAD_DIM), lambda i: (0, i, 0)),
                   pl.BlockSpec((n_heads, tm), lambda i: (0, i))),
        scratch_shapes=[pltpu.VMEM((tm, d), BF16)],
        compiler_params=pltpu.CompilerParams(
            dimension_semantics=("arbitrary",), vmem_limit_bytes=VMEM_LIMIT),
        name="in_proj",
    )(h, g, w_t)


def _scan_kernel(fl_ref, fll_ref, b_ref, nc_ref, ncl_ref, *, batch, seq):
    n_heads = fl_ref.shape[0]
    lane = lax.broadcasted_iota(jnp.int32, (n_heads, LANES), 1)
    bias = b_ref[...]

    def log_f(v):
        v = v + bias
        return jnp.minimum(v, 0.0) - jnp.log(1.0 + jnp.exp(-jnp.abs(v)))

    def scan_lanes(v):
        shift = 1
        while shift < LANES:
            v = v + jnp.where(lane >= shift, pltpu.roll(v, shift, axis=1), 0.0)
            shift *= 2
        return v

    c_lead = scan_lanes(jnp.where(lane >= META_BLOCK - N_META, log_f(fll_ref[...]), 0.0))
    ncl_ref[...] = -LOG2E * c_lead

    def body(k, carries):
        out = []
        for b in range(batch):
            start = pl.multiple_of(b * seq + k * LANES, LANES)
            c = scan_lanes(log_f(fl_ref[:, pl.ds(start, LANES)])) + carries[b]
            nc_ref[:, pl.ds(start, LANES)] = -LOG2E * c
            out.append(c[:, LANES - 1:LANES])
        return tuple(out)

    lax.fori_loop(0, seq // LANES, body, (c_lead[:, LANES - 1:LANES],) * batch)


def _scan(fl, fl_lead, b_forget, *, batch, seq):
    n_heads = fl.shape[0]
    return pl.pallas_call(
        functools.partial(_scan_kernel, batch=batch, seq=seq),
        out_shape=(jax.ShapeDtypeStruct(fl.shape, F32), jax.ShapeDtypeStruct(fl_lead.shape, F32)),
        name="forget_scan",
    )(fl, fl_lead, b_forget.reshape(n_heads, 1))


def _fox_step(q, k, v, nc, carry, mask):
    m, l, acc = carry
    s = _dot_nt(q, k) + nc
    if mask is not None:
        s = jnp.where(mask, s, NEG_INF)
    m_new = jnp.maximum(m, jnp.max(s, axis=-1, keepdims=True))
    alpha = jnp.exp2(m - m_new)
    p = jnp.exp2(s - m_new)
    l = alpha * l + jnp.sum(p, axis=-1, keepdims=True)
    acc = alpha * acc + _dot(p.astype(BF16), v)
    return m_new, l, acc


def _sb_weights(q, k, mask, suffix):
    z = _dot_nt(q, k)
    nz = -z
    l1m = jnp.minimum(nz, 0.0) - jnp.log(1.0 + jnp.exp2(jnp.minimum(z, nz))) * LOG2E
    if mask is not None:
        l1m = jnp.where(mask, l1m, 0.0)
    incl = _dot(l1m.astype(BF16), suffix)
    e = z + incl
    if mask is not None:
        e = jnp.where(mask, e, NEG_INF)
    return jnp.exp2(e).astype(BF16), incl[:, :1]


def _sb_block(q, k, v, r, acc, mask, suffix):
    a, total = _sb_weights(q, k, mask, suffix)
    return r + total, acc + jnp.exp2(r) * _dot(a, v)


def _mixer_kernel(qf_ref, kf_ref, vf_ref, qs_ref, ks_ref, vs_ref, kfm_ref, vfm_ref, ksm_ref, vsm_ref,
                  ncx_ref, ncm_ref, gf_ref, gs_ref, of_ref, os_ref, *, tq, seq):
    n_sub = tq // SB_KEYS
    fox_keys = min(tq, FOX_DIAG_KEYS)
    row = lax.broadcasted_iota(jnp.int32, (tq, SB_KEYS), 0)
    col = lax.broadcasted_iota(jnp.int32, (tq, SB_KEYS), 1)
    strict = col < row
    causal = (lax.broadcasted_iota(jnp.int32, (tq, fox_keys), 1)
              <= lax.broadcasted_iota(jnp.int32, (tq, fox_keys), 0))
    urow = lax.broadcasted_iota(jnp.int32, (SB_KEYS, SB_KEYS), 0)
    ucol = lax.broadcasted_iota(jnp.int32, (SB_KEYS, SB_KEYS), 1)
    suffix = (urow >= ucol).astype(BF16)
    mrow = lax.broadcasted_iota(jnp.int32, (META_BLOCK, META_BLOCK), 0)
    mcol = lax.broadcasted_iota(jnp.int32, (META_BLOCK, META_BLOCK), 1)
    suffix_m = (mrow >= mcol).astype(BF16)
    meta_valid = lax.broadcasted_iota(jnp.int32, (1, META_BLOCK), 1) >= META_BLOCK - N_META
    meta_bias = jnp.where(meta_valid, ncm_ref[...], NEG_INF)

    def q_tile(t, _):
        q0 = pl.multiple_of(t * tq, tq)
        qf = qf_ref[pl.ds(q0, tq), :]
        qs = qs_ref[pl.ds(q0, tq), :]

        s = _dot_nt(qf, kfm_ref[...]) + meta_bias
        m = jnp.max(s, axis=-1, keepdims=True)
        p = jnp.exp2(s - m)
        fox = (m, jnp.sum(p, axis=-1, keepdims=True), _dot(p.astype(BF16), vfm_ref[...]))

        sb = None
        for d in reversed(range(n_sub)):
            r0 = d * SB_KEYS
            k0 = pl.multiple_of(q0 + r0, SB_KEYS)
            if sb is None:
                r, acc = jnp.zeros((tq - r0, 1), F32), jnp.zeros((tq - r0, HEAD_DIM), F32)
            else:
                r = jnp.concatenate([jnp.zeros((SB_KEYS, 1), F32), sb[0]], axis=0)
                acc = jnp.concatenate([jnp.zeros((SB_KEYS, HEAD_DIM), F32), sb[1]], axis=0)
            sb = _sb_block(qs[r0:], ks_ref[pl.ds(k0, SB_KEYS), :], vs_ref[pl.ds(k0, SB_KEYS), :],
                           r, acc, strict[:tq - r0], suffix)

        def kv_step(jj, carry):
            fox, sb = carry
            for d in range(tq // fox_keys):
                k0 = pl.multiple_of(jj * tq + d * fox_keys, fox_keys)
                fox = _fox_step(qf, kf_ref[pl.ds(k0, fox_keys), :], vf_ref[pl.ds(k0, fox_keys), :],
                                ncx_ref[:, pl.ds(k0, fox_keys)], fox, None)
            for d in reversed(range(n_sub)):
                k0 = pl.multiple_of((t - 1 - jj) * tq + d * SB_KEYS, SB_KEYS)
                sb = _sb_block(qs, ks_ref[pl.ds(k0, SB_KEYS), :], vs_ref[pl.ds(k0, SB_KEYS), :],
                               sb[0], sb[1], None, suffix)
            return fox, sb

        fox, sb = lax.fori_loop(0, t, kv_step, (fox, sb))

        for d in range(tq // fox_keys):
            r0 = d * fox_keys
            k0 = pl.multiple_of(q0 + r0, fox_keys)
            sub = _fox_step(qf[r0:], kf_ref[pl.ds(k0, fox_keys), :], vf_ref[pl.ds(k0, fox_keys), :],
                            ncx_ref[:, pl.ds(k0, fox_keys)], tuple(a[r0:] for a in fox), causal[:tq - r0])
            fox = tuple(jnp.concatenate([a[:r0], b], axis=0) for a, b in zip(fox, sub)) if r0 else sub
        sb = _sb_block(qs, ksm_ref[...], vsm_ref[...], sb[0], sb[1], meta_valid, suffix_m)

        of_ref[pl.ds(q0, tq), :] = _head_norm(fox[2] / fox[1], gf_ref[...]).astype(BF16)
        os_ref[pl.ds(q0, tq), :] = _head_norm(sb[1], gs_ref[...]).astype(BF16)
        return 0

    lax.fori_loop(0, seq // tq, q_tile, 0)


def _attention(qkv, qkv_lead, nc, nc_lead, g_fox, g_sb, *, batch, seq, n_heads):
    H = n_heads
    hd = H * HEAD_DIM
    tq = _row_tile(seq, ATTN_ROWS)
    assert tq % SB_KEYS == 0 and tq % min(tq, FOX_DIAG_KEYS) == 0

    def tok(base):
        return pl.BlockSpec((None, seq, HEAD_DIM), lambda b, h: (base + h, b, 0))

    def meta(base):
        return pl.BlockSpec((None, META_BLOCK, HEAD_DIM), lambda b, h: (base + h, 0, 0))

    gain = pl.BlockSpec((1, HEAD_DIM), lambda b, h: (0, h))
    out_spec = pl.BlockSpec((seq, HEAD_DIM), lambda b, h: (b, h))
    out_shape = jax.ShapeDtypeStruct((batch * seq, hd), BF16)
    return pl.pallas_call(
        functools.partial(_mixer_kernel, tq=tq, seq=seq),
        out_shape=(out_shape, out_shape),
        grid=(batch, H),
        in_specs=[tok(0), tok(H), tok(2 * H), tok(3 * H), tok(4 * H), tok(5 * H),
                  meta(H), meta(2 * H), meta(4 * H), meta(5 * H),
                  pl.BlockSpec((None, 1, seq), lambda b, h: (h, 0, b)),
                  pl.BlockSpec((None, 1, META_BLOCK), lambda b, h: (h, 0, 0)),
                  gain, gain],
        out_specs=(out_spec, out_spec),
        compiler_params=pltpu.CompilerParams(
            dimension_semantics=("arbitrary", "arbitrary"), vmem_limit_bytes=VMEM_LIMIT),
        name="mixer_attention",
    )(qkv, qkv, qkv, qkv, qkv, qkv, qkv_lead, qkv_lead, qkv_lead, qkv_lead,
      nc.reshape(H, 1, -1), nc_lead.reshape(H, 1, -1), g_fox, g_sb)


def _meta_attn_kernel(qkv_ref, ncm_ref, g_ref, o_ref, *, n_heads):
    H = n_heads
    row = lax.broadcasted_iota(jnp.int32, (META_BLOCK, META_BLOCK), 0)
    col = lax.broadcasted_iota(jnp.int32, (META_BLOCK, META_BLOCK), 1)
    valid = col >= META_BLOCK - N_META
    fox_allowed = (col <= row) & (valid | (col == row))
    sb_allowed = (col < row) & valid
    suffix = (row >= col).astype(BF16)
    for hh in range(H):
        s = jnp.where(fox_allowed, _dot_nt(qkv_ref[hh], qkv_ref[H + hh]) + ncm_ref[hh], NEG_INF)
        p = jnp.exp2(s - jnp.max(s, axis=-1, keepdims=True))
        o = _dot(p.astype(BF16), qkv_ref[2 * H + hh]) / jnp.sum(p, axis=-1, keepdims=True)
        cols = slice(hh * HEAD_DIM, (hh + 1) * HEAD_DIM)
        o_ref[:, cols] = _head_norm(o, g_ref[:, cols]).astype(BF16)
    for hh in range(H):
        a, _ = _sb_weights(qkv_ref[3 * H + hh], qkv_ref[4 * H + hh], sb_allowed, suffix)
        cols = slice((H + hh) * HEAD_DIM, (H + hh + 1) * HEAD_DIM)
        o_ref[:, cols] = _head_norm(_dot(a, qkv_ref[5 * H + hh]), g_ref[:, cols]).astype(BF16)


def _meta_attention(qkv_lead, nc_lead, g_cat, *, n_heads):
    H = n_heads
    return pl.pallas_call(
        functools.partial(_meta_attn_kernel, n_heads=H),
        out_shape=jax.ShapeDtypeStruct((META_BLOCK, 2 * H * HEAD_DIM), BF16),
        name="meta_attention",
    )(qkv_lead, nc_lead.reshape(H, 1, -1), g_cat)


def _out_proj_kernel(h_ref, *rest, widths):
    *o_refs, w_ref, out_ref = rest
    acc = h_ref[...]
    start = 0
    for o_ref, width in zip(o_refs, widths):
        acc = acc + _dot(o_ref[...], w_ref[start:start + width, :])
        start += width
    out_ref[...] = acc


def _out_proj(h, parts, w, *, layer):
    rows, d = h.shape
    tm = _row_tile(rows, PROJ_ROWS)
    widths = tuple(p.shape[1] for p in parts)
    assert sum(widths) == w.shape[1]
    return pl.pallas_call(
        functools.partial(_out_proj_kernel, widths=widths),
        out_shape=jax.ShapeDtypeStruct((rows, d), F32),
        grid=(rows // tm,),
        in_specs=[pl.BlockSpec((tm, d), lambda i: (i, 0))]
        + [pl.BlockSpec((tm, width), lambda i: (i, 0)) for width in widths]
        + [pl.BlockSpec((None,) + w.shape[1:], lambda i: (layer, 0, 0))],
        out_specs=pl.BlockSpec((tm, d), lambda i: (i, 0)),
        compiler_params=pltpu.CompilerParams(
            dimension_semantics=("arbitrary",), vmem_limit_bytes=VMEM_LIMIT),
        name="out_proj",
    )(h, *parts, w)


def kernel(x, meta_tokens, ffn1_norm, ffn1_w_gate, ffn1_w_up, ffn1_w_down, mix_norm, w_in, b_forget, g_fox, g_sb, w_out, ffn2_norm, ffn2_w_gate, ffn2_w_up, ffn2_w_down, final_norm):
    batch, seq, d = x.shape
    depth = w_in.shape[0]
    n_heads = g_fox.shape[1] // HEAD_DIM
    hd = n_heads * HEAD_DIM
    f = ffn1_w_gate.shape[2]
    assert meta_tokens.shape[0] == N_META and w_in.shape[2] == 6 * hd + n_heads
    tf = 512 if f % 512 == 0 else LANES

    tok = x.reshape(batch * seq, d)
    lead = jnp.concatenate([jnp.zeros((META_BLOCK - N_META, d), x.dtype), meta_tokens.astype(x.dtype)], axis=0)

    row = lambda v: v.reshape(1, -1)
    w1g, w1u, w1d = _to_bf16(ffn1_w_gate), _to_bf16(ffn1_w_up), _to_bf16(ffn1_w_down)
    w2g, w2u, w2d = _to_bf16(ffn2_w_gate), _to_bf16(ffn2_w_up), _to_bf16(ffn2_w_down)
    w_in_t = _to_bf16(jnp.swapaxes(w_in, 1, 2))
    w_o = _to_bf16(w_out)
    for l in range(depth):
        last = l == depth - 1
        tok = _ffn(tok, row(ffn1_norm[l]), w1g, w1u, w1d, layer=l, tf=tf)
        lead = _ffn(lead, row(ffn1_norm[l]), w1g, w1u, w1d, layer=l, tf=tf)

        qkv, fl = _proj(tok, row(mix_norm[l]), w_in_t, layer=l, n_heads=n_heads)
        qkv_lead, fl_lead = _proj(lead, row(mix_norm[l]), w_in_t, layer=l, n_heads=n_heads)
        nc, nc_lead = _scan(fl, fl_lead, b_forget[l], batch=batch, seq=seq)
        o_f, o_s = _attention(qkv, qkv_lead, nc, nc_lead, row(g_fox[l]), row(g_sb[l]),
                              batch=batch, seq=seq, n_heads=n_heads)
        tok = _out_proj(tok, [o_f, o_s], w_o, layer=l)
        if not last:
            o_m = _meta_attention(qkv_lead, nc_lead, row(jnp.concatenate([g_fox[l], g_sb[l]])), n_heads=n_heads)
            lead = _out_proj(lead, [o_m], w_o, layer=l)
            lead = _ffn(lead, row(ffn2_norm[l]), w2g, w2u, w2d, layer=l, tf=tf)

        tok = _ffn(tok, row(ffn2_norm[l]), w2g, w2u, w2d, layer=l, tf=tf,
                   final_gain=row(final_norm) if last else None)
    return tok.reshape(batch, seq, d)
```

```python
import functools

import jax
import jax.numpy as jnp
from jax import lax
from jax.experimental import pallas as pl
from jax.experimental.pallas import tpu as pltpu

N_META = 16
HEAD_DIM = 128
META_BLOCK = 128
EPS = 1e-6
LANES = 128
SB_KEYS = 256
FOX_DIAG_KEYS = 512
LOG2E = 1.4426950408889634
FFN_ROWS = 1024
PROJ_ROWS = 512
ATTN_ROWS = 1024
CAST_BLOCK_BYTES = 4 * 1024 * 1024
VMEM_LIMIT = 60 * 1024 * 1024

F32 = jnp.float32
BF16 = jnp.bfloat16
NEG_INF = float("-inf")


def _dot(a, b):
    return jnp.dot(a, b, preferred_element_type=F32)


def _dot_nt(a, b):
    return lax.dot_general(a, b, (((1,), (1,)), ((), ())), preferred_element_type=F32)


def _rms(x, g):
    return x * lax.rsqrt(jnp.mean(x * x, axis=-1, keepdims=True) + EPS) * g


def _head_norm(o, g):
    return o * lax.rsqrt(jnp.mean(o * o, axis=-1, keepdims=True) + EPS) * g


def _row_tile(rows, want):
    tile = min(rows, want)
    assert rows % tile == 0
    return tile


def _cast_kernel(x_ref, o_ref):
    o_ref[...] = x_ref[...].astype(BF16)


def _largest_divisor(n, step, limit):
    best = None
    for cand in range(step, min(n, limit) + 1, step):
        if n % cand == 0:
            best = cand
    return best


def _to_bf16(w):
    depth, r, c = w.shape
    br = _largest_divisor(r, 8, CAST_BLOCK_BYTES // (4 * c))
    if br is not None and 4 * br * c * 4 >= CAST_BLOCK_BYTES:
        block = (None, br, c)
    else:
        block = (None, r, _largest_divisor(c, LANES, max(LANES, CAST_BLOCK_BYTES // (4 * r))))
    spec = pl.BlockSpec(block, lambda l, i, j: (l, i, j))
    return pl.pallas_call(
        _cast_kernel,
        out_shape=jax.ShapeDtypeStruct(w.shape, BF16),
        grid=(depth, r // block[1], c // block[2]),
        in_specs=[spec],
        out_specs=spec,
        compiler_params=pltpu.CompilerParams(
            dimension_semantics=("arbitrary",) * 3, vmem_limit_bytes=VMEM_LIMIT),
        name="cast_bf16",
    )(w)


def _ffn_kernel(h_ref, g_ref, wg_ref, wu_ref, wd_ref, *rest, final):
    if final:
        gf_ref, o_ref, xn_ref = rest
    else:
        o_ref, xn_ref = rest
    j = pl.program_id(1)

    @pl.when(j == 0)
    def _():
        h = h_ref[...]
        xn_ref[...] = _rms(h, g_ref[...]).astype(BF16)
        o_ref[...] = h

    xn = xn_ref[...]
    gate = _dot(xn, wg_ref[...])
    up = _dot(xn, wu_ref[...])
    a = (gate * jax.nn.sigmoid(gate) * up * 0.5).astype(BF16)
    o_ref[...] += _dot(a, wd_ref[...])

    if final:
        @pl.when(j == pl.num_programs(1) - 1)
        def _():
            o_ref[...] = _rms(o_ref[...], gf_ref[...])


def _ffn(h, g, wg, wu, wd, *, layer, tf, final_gain=None):
    rows, d = h.shape
    f = wg.shape[2]
    tm = _row_tile(rows, FFN_ROWS)
    final = final_gain is not None
    in_specs = [
        pl.BlockSpec((tm, d), lambda i, j: (i, 0)),
        pl.BlockSpec((1, d), lambda i, j: (0, 0)),
        pl.BlockSpec((None, d, tf), lambda i, j: (layer, 0, j)),
        pl.BlockSpec((None, d, tf), lambda i, j: (layer, 0, j)),
        pl.BlockSpec((None, tf, d), lambda i, j: (layer, j, 0)),
    ]
    args = [h, g, wg, wu, wd]
    if final:
        in_specs.append(pl.BlockSpec((1, d), lambda i, j: (0, 0)))
        args.append(final_gain)
    return pl.pallas_call(
        functools.partial(_ffn_kernel, final=final),
        out_shape=jax.ShapeDtypeStruct((rows, d), F32),
        grid=(rows // tm, f // tf),
        in_specs=in_specs,
        out_specs=pl.BlockSpec((tm, d), lambda i, j: (i, 0)),
        scratch_shapes=[pltpu.VMEM((tm, d), BF16)],
        compiler_params=pltpu.CompilerParams(
            dimension_semantics=("arbitrary", "arbitrary"), vmem_limit_bytes=VMEM_LIMIT),
        name="ffn_final" if final else "ffn",
    )(*args)


def _proj_kernel(h_ref, g_ref, w_ref, qkv_ref, fl_ref, xn_ref, *, n_heads, q_scale):
    hd = n_heads * HEAD_DIM
    xn_ref[...] = _rms(h_ref[...], g_ref[...]).astype(BF16)
    fl_ref[...] = _dot_nt(w_ref[6 * hd:, :], xn_ref[...])
    for j in range(6):
        acc = _dot_nt(xn_ref[...], w_ref[j * hd:(j + 1) * hd, :])
        if j in (0, 3):
            acc = acc * q_scale
        for hh in range(n_heads):
            qkv_ref[j * n_heads + hh] = acc[:, hh * HEAD_DIM:(hh + 1) * HEAD_DIM].astype(BF16)


def _proj(h, g, w_t, *, layer, n_heads):
    rows, d = h.shape
    tm = _row_tile(rows, PROJ_ROWS)
    return pl.pallas_call(
        functools.partial(_proj_kernel, n_heads=n_heads, q_scale=LOG2E * HEAD_DIM ** -0.5),
        out_shape=(jax.ShapeDtypeStruct((6 * n_heads, rows, HEAD_DIM), BF16),
                   jax.ShapeDtypeStruct((n_heads, rows), F32)),
        grid=(rows // tm,),
        in_specs=[
            pl.BlockSpec((tm, d), lambda i: (i, 0)),
            pl.BlockSpec((1, d), lambda i: (0, 0)),
            pl.BlockSpec((None,) + w_t.shape[1:], lambda i: (layer, 0, 0), pipeline_mode=pl.Buffered(1)),
        ],
        out_specs=(pl.BlockSpec((6 * n_heads, tm, HEAD_DIM), lambda i: (0, i, 0)),
                   pl.BlockSpec((n_heads, tm), lambda i: (0, i))),
        scratch_shapes=[pltpu.VMEM((tm, d), BF16)],
        compiler_params=pltpu.CompilerParams(
            dimension_semantics=("arbitrary",), vmem_limit_bytes=VMEM_LIMIT),
        name="in_proj",
    )(h, g, w_t)


def _scan_kernel(fl_ref, fll_ref, b_ref, nc_ref, ncl_ref, *, batch, seq):
    n_heads = fl_ref.shape[0]
    lane = lax.broadcasted_iota(jnp.int32, (n_heads, LANES), 1)
    bias = b_ref[...]

    def log_f(v):
        v = v + bias
        return jnp.minimum(v, 0.0) - jnp.log(1.0 + jnp.exp(-jnp.abs(v)))

    def scan_lanes(v):
        shift = 1
        while shift < LANES:
            v = v + jnp.where(lane >= shift, pltpu.roll(v, shift, axis=1), 0.0)
            shift *= 2
        return v

    c_lead = scan_lanes(jnp.where(lane >= META_BLOCK - N_META, log_f(fll_ref[...]), 0.0))
    ncl_ref[...] = -LOG2E * c_lead

    def body(k, carries):
        out = []
        for b in range(batch):
            start = pl.multiple_of(b * seq + k * LANES, LANES)
            c = scan_lanes(log_f(fl_ref[:, pl.ds(start, LANES)])) + carries[b]
            nc_ref[:, pl.ds(start, LANES)] = -LOG2E * c
            out.append(c[:, LANES - 1:LANES])
        return tuple(out)

    lax.fori_loop(0, seq // LANES, body, (c_lead[:, LANES - 1:LANES],) * batch)


def _scan(fl, fl_lead, b_forget, *, batch, seq):
    n_heads = fl.shape[0]
    return pl.pallas_call(
        functools.partial(_scan_kernel, batch=batch, seq=seq),
        out_shape=(jax.ShapeDtypeStruct(fl.shape, F32), jax.ShapeDtypeStruct(fl_lead.shape, F32)),
        name="forget_scan",
    )(fl, fl_lead, b_forget.reshape(n_heads, 1))


def _fox_step(q, k, v, nc, carry, mask):
    m, l, acc = carry
    s = _dot_nt(q, k) + nc
    if mask is not None:
        s = jnp.where(mask, s, NEG_INF)
    m_new = jnp.maximum(m, jnp.max(s, axis=-1, keepdims=True))
    alpha = jnp.exp2(m - m_new)
    p = jnp.exp2(s - m_new)
    l = alpha * l + jnp.sum(p, axis=-1, keepdims=True)
    acc = alpha * acc + _dot(p.astype(BF16), v)
    return m_new, l, acc


def _sb_weights(q, k, mask, suffix):
    z = _dot_nt(q, k)
    nz = -z
    l1m = jnp.minimum(nz, 0.0) - jnp.log(1.0 + jnp.exp2(jnp.minimum(z, nz))) * LOG2E
    if mask is not None:
        l1m = jnp.where(mask, l1m, 0.0)
    incl = _dot(l1m.astype(BF16), suffix)
    e = z + incl
    if mask is not None:
        e = jnp.where(mask, e, NEG_INF)
    return jnp.exp2(e).astype(BF16), incl[:, :1]


def _sb_block(q, k, v, r, acc, mask, suffix):
    a, total = _sb_weights(q, k, mask, suffix)
    return r + total, acc + jnp.exp2(r) * _dot(a, v)


def _mixer_kernel(qf_ref, kf_ref, vf_ref, qs_ref, ks_ref, vs_ref, kfm_ref, vfm_ref, ksm_ref, vsm_ref,
                  ncx_ref, ncm_ref, gf_ref, gs_ref, *rest, tq, seq):
    n_cast = (len(rest) - 2) // 2
    of_ref, os_ref = rest[n_cast:n_cast + 2]
    for src_ref, dst_ref in zip(rest[:n_cast], rest[n_cast + 2:]):
        dst_ref[...] = src_ref[...].astype(BF16)

    n_sub = tq // SB_KEYS
    fox_keys = min(tq, FOX_DIAG_KEYS)
    row = lax.broadcasted_iota(jnp.int32, (tq, SB_KEYS), 0)
    col = lax.broadcasted_iota(jnp.int32, (tq, SB_KEYS), 1)
    strict = col < row
    causal = (lax.broadcasted_iota(jnp.int32, (tq, fox_keys), 1)
              <= lax.broadcasted_iota(jnp.int32, (tq, fox_keys), 0))
    urow = lax.broadcasted_iota(jnp.int32, (SB_KEYS, SB_KEYS), 0)
    ucol = lax.broadcasted_iota(jnp.int32, (SB_KEYS, SB_KEYS), 1)
    suffix = (urow >= ucol).astype(BF16)
    mrow = lax.broadcasted_iota(jnp.int32, (META_BLOCK, META_BLOCK), 0)
    mcol = lax.broadcasted_iota(jnp.int32, (META_BLOCK, META_BLOCK), 1)
    suffix_m = (mrow >= mcol).astype(BF16)
    meta_valid = lax.broadcasted_iota(jnp.int32, (1, META_BLOCK), 1) >= META_BLOCK - N_META
    meta_bias = jnp.where(meta_valid, ncm_ref[...], NEG_INF)

    def q_tile(t, _):
        q0 = pl.multiple_of(t * tq, tq)
        qf = qf_ref[pl.ds(q0, tq), :]
        qs = qs_ref[pl.ds(q0, tq), :]

        s = _dot_nt(qf, kfm_ref[...]) + meta_bias
        m = jnp.max(s, axis=-1, keepdims=True)
        p = jnp.exp2(s - m)
        fox = (m, jnp.sum(p, axis=-1, keepdims=True), _dot(p.astype(BF16), vfm_ref[...]))

        sb = None
        for d in reversed(range(n_sub)):
            r0 = d * SB_KEYS
            k0 = pl.multiple_of(q0 + r0, SB_KEYS)
            if sb is None:
                r, acc = jnp.zeros((tq - r0, 1), F32), jnp.zeros((tq - r0, HEAD_DIM), F32)
            else:
                r = jnp.concatenate([jnp.zeros((SB_KEYS, 1), F32), sb[0]], axis=0)
                acc = jnp.concatenate([jnp.zeros((SB_KEYS, HEAD_DIM), F32), sb[1]], axis=0)
            sb = _sb_block(qs[r0:], ks_ref[pl.ds(k0, SB_KEYS), :], vs_ref[pl.ds(k0, SB_KEYS), :],
                           r, acc, strict[:tq - r0], suffix)

        def kv_step(jj, carry):
            fox, sb = carry
            for d in range(tq // fox_keys):
                k0 = pl.multiple_of(jj * tq + d * fox_keys, fox_keys)
                fox = _fox_step(qf, kf_ref[pl.ds(k0, fox_keys), :], vf_ref[pl.ds(k0, fox_keys), :],
                                ncx_ref[:, pl.ds(k0, fox_keys)], fox, None)
            for d in reversed(range(n_sub)):
                k0 = pl.multiple_of((t - 1 - jj) * tq + d * SB_KEYS, SB_KEYS)
                sb = _sb_block(qs, ks_ref[pl.ds(k0, SB_KEYS), :], vs_ref[pl.ds(k0, SB_KEYS), :],
                               sb[0], sb[1], None, suffix)
            return fox, sb

        fox, sb = lax.fori_loop(0, t, kv_step, (fox, sb))

        for d in range(tq // fox_keys):
            r0 = d * fox_keys
            k0 = pl.multiple_of(q0 + r0, fox_keys)
            sub = _fox_step(qf[r0:], kf_ref[pl.ds(k0, fox_keys), :], vf_ref[pl.ds(k0, fox_keys), :],
                            ncx_ref[:, pl.ds(k0, fox_keys)], tuple(a[r0:] for a in fox), causal[:tq - r0])
            fox = tuple(jnp.concatenate([a[:r0], b], axis=0) for a, b in zip(fox, sub)) if r0 else sub
        sb = _sb_block(qs, ksm_ref[...], vsm_ref[...], sb[0], sb[1], meta_valid, suffix_m)

        of_ref[pl.ds(q0, tq), :] = _head_norm(fox[2] / fox[1], gf_ref[...]).astype(BF16)
        os_ref[pl.ds(q0, tq), :] = _head_norm(sb[1], gs_ref[...]).astype(BF16)
        return 0

    lax.fori_loop(0, seq // tq, q_tile, 0)


def _attention(qkv, qkv_lead, nc, nc_lead, g_fox, g_sb, *, batch, seq, n_heads, cast_along=()):
    H = n_heads
    hd = H * HEAD_DIM
    tq = _row_tile(seq, ATTN_ROWS)
    assert tq % SB_KEYS == 0 and tq % min(tq, FOX_DIAG_KEYS) == 0
    cast_in, cast_specs, cast_shapes = [], [], []
    for w in cast_along:
        rows, c = w.shape[0] * w.shape[1], w.shape[2]
        rb = rows // (batch * H)
        assert rb % 16 == 0 and rb * batch * H == rows
        cast_in.append(w.reshape(rows, c))
        cast_specs.append(pl.BlockSpec((rb, c), lambda b, h: (b * H + h, 0)))
        cast_shapes.append(jax.ShapeDtypeStruct((rows, c), BF16))

    def tok(base):
        return pl.BlockSpec((None, seq, HEAD_DIM), lambda b, h: (base + h, b, 0))

    def meta(base):
        return pl.BlockSpec((None, META_BLOCK, HEAD_DIM), lambda b, h: (base + h, 0, 0))

    gain = pl.BlockSpec((1, HEAD_DIM), lambda b, h: (0, h))
    out_spec = pl.BlockSpec((seq, HEAD_DIM), lambda b, h: (b, h))
    out_shape = jax.ShapeDtypeStruct((batch * seq, hd), BF16)
    o_f, o_s, *cast_out = pl.pallas_call(
        functools.partial(_mixer_kernel, tq=tq, seq=seq),
        out_shape=(out_shape, out_shape, *cast_shapes),
        grid=(batch, H),
        in_specs=[tok(0), tok(H), tok(2 * H), tok(3 * H), tok(4 * H), tok(5 * H),
                  meta(H), meta(2 * H), meta(4 * H), meta(5 * H),
                  pl.BlockSpec((None, 1, seq), lambda b, h: (h, 0, b)),
                  pl.BlockSpec((None, 1, META_BLOCK), lambda b, h: (h, 0, 0)),
                  gain, gain, *cast_specs],
        out_specs=(out_spec, out_spec, *cast_specs),
        compiler_params=pltpu.CompilerParams(
            dimension_semantics=("arbitrary", "arbitrary"), vmem_limit_bytes=VMEM_LIMIT),
        name="mixer_attention",
    )(qkv, qkv, qkv, qkv, qkv, qkv, qkv_lead, qkv_lead, qkv_lead, qkv_lead,
      nc.reshape(H, 1, -1), nc_lead.reshape(H, 1, -1), g_fox, g_sb, *cast_in)
    return o_f, o_s, [o.reshape(w.shape) for o, w in zip(cast_out, cast_along)]


def _meta_attn_kernel(qkv_ref, ncm_ref, g_ref, o_ref, *, n_heads):
    H = n_heads
    row = lax.broadcasted_iota(jnp.int32, (META_BLOCK, META_BLOCK), 0)
    col = lax.broadcasted_iota(jnp.int32, (META_BLOCK, META_BLOCK), 1)
    valid = col >= META_BLOCK - N_META
    fox_allowed = (col <= row) & (valid | (col == row))
    sb_allowed = (col < row) & valid
    suffix = (row >= col).astype(BF16)
    for hh in range(H):
        s = jnp.where(fox_allowed, _dot_nt(qkv_ref[hh], qkv_ref[H + hh]) + ncm_ref[hh], NEG_INF)
        p = jnp.exp2(s - jnp.max(s, axis=-1, keepdims=True))
        o = _dot(p.astype(BF16), qkv_ref[2 * H + hh]) / jnp.sum(p, axis=-1, keepdims=True)
        cols = slice(hh * HEAD_DIM, (hh + 1) * HEAD_DIM)
        o_ref[:, cols] = _head_norm(o, g_ref[:, cols]).astype(BF16)
    for hh in range(H):
        a, _ = _sb_weights(qkv_ref[3 * H + hh], qkv_ref[4 * H + hh], sb_allowed, suffix)
        cols = slice((H + hh) * HEAD_DIM, (H + hh + 1) * HEAD_DIM)
        o_ref[:, cols] = _head_norm(_dot(a, qkv_ref[5 * H + hh]), g_ref[:, cols]).astype(BF16)


def _meta_attention(qkv_lead, nc_lead, g_cat, *, n_heads):
    H = n_heads
    return pl.pallas_call(
        functools.partial(_meta_attn_kernel, n_heads=H),
        out_shape=jax.ShapeDtypeStruct((META_BLOCK, 2 * H * HEAD_DIM), BF16),
        name="meta_attention",
    )(qkv_lead, nc_lead.reshape(H, 1, -1), g_cat)


def _out_proj_kernel(h_ref, *rest, widths):
    *o_refs, w_ref, out_ref = rest
    acc = h_ref[...]
    start = 0
    for o_ref, width in zip(o_refs, widths):
        acc = acc + _dot(o_ref[...], w_ref[start:start + width, :])
        start += width
    out_ref[...] = acc


def _out_proj(h, parts, w, *, layer):
    rows, d = h.shape
    tm = _row_tile(rows, PROJ_ROWS)
    widths = tuple(p.shape[1] for p in parts)
    assert sum(widths) == w.shape[1]
    return pl.pallas_call(
        functools.partial(_out_proj_kernel, widths=widths),
        out_shape=jax.ShapeDtypeStruct((rows, d), F32),
        grid=(rows // tm,),
        in_specs=[pl.BlockSpec((tm, d), lambda i: (i, 0))]
        + [pl.BlockSpec((tm, width), lambda i: (i, 0)) for width in widths]
        + [pl.BlockSpec((None,) + w.shape[1:], lambda i: (layer, 0, 0))],
        out_specs=pl.BlockSpec((tm, d), lambda i: (i, 0)),
        compiler_params=pltpu.CompilerParams(
            dimension_semantics=("arbitrary",), vmem_limit_bytes=VMEM_LIMIT),
        name="out_proj",
    )(h, *parts, w)


def kernel(x, meta_tokens, ffn1_norm, ffn1_w_gate, ffn1_w_up, ffn1_w_down, mix_norm, w_in, b_forget, g_fox, g_sb, w_out, ffn2_norm, ffn2_w_gate, ffn2_w_up, ffn2_w_down, final_norm):
    batch, seq, d = x.shape
    depth = w_in.shape[0]
    n_heads = g_fox.shape[1] // HEAD_DIM
    hd = n_heads * HEAD_DIM
    f = ffn1_w_gate.shape[2]
    assert meta_tokens.shape[0] == N_META and w_in.shape[2] == 6 * hd + n_heads
    tf = 512 if f % 512 == 0 else LANES

    tok = x.reshape(batch * seq, d)
    lead = jnp.concatenate([jnp.zeros((META_BLOCK - N_META, d), x.dtype), meta_tokens.astype(x.dtype)], axis=0)

    row = lambda v: v.reshape(1, -1)
    w1g, w1u, w1d = _to_bf16(ffn1_w_gate), _to_bf16(ffn1_w_up), _to_bf16(ffn1_w_down)
    w_in_t = _to_bf16(jnp.swapaxes(w_in, 1, 2))
    w_o = _to_bf16(w_out)
    for l in range(depth):
        last = l == depth - 1
        tok = _ffn(tok, row(ffn1_norm[l]), w1g, w1u, w1d, layer=l, tf=tf)
        lead = _ffn(lead, row(ffn1_norm[l]), w1g, w1u, w1d, layer=l, tf=tf)

        qkv, fl = _proj(tok, row(mix_norm[l]), w_in_t, layer=l, n_heads=n_heads)
        qkv_lead, fl_lead = _proj(lead, row(mix_norm[l]), w_in_t, layer=l, n_heads=n_heads)
        nc, nc_lead = _scan(fl, fl_lead, b_forget[l], batch=batch, seq=seq)
        o_f, o_s, cast = _attention(qkv, qkv_lead, nc, nc_lead, row(g_fox[l]), row(g_sb[l]),
                                    batch=batch, seq=seq, n_heads=n_heads,
                                    cast_along=(ffn2_w_gate, ffn2_w_up, ffn2_w_down) if l == 0 else ())
        if l == 0:
            w2g, w2u, w2d = cast
        tok = _out_proj(tok, [o_f, o_s], w_o, layer=l)
        if not last:
            o_m = _meta_attention(qkv_lead, nc_lead, row(jnp.concatenate([g_fox[l], g_sb[l]])), n_heads=n_heads)
            lead = _out_proj(lead, [o_m], w_o, layer=l)
            lead = _ffn(lead, row(ffn2_norm[l]), w2g, w2u, w2d, layer=l, tf=tf)

        tok = _ffn(tok, row(ffn2_norm[l]), w2g, w2u, w2d, layer=l, tf=tf,
                   final_gain=row(final_norm) if last else None)
    return tok.reshape(batch, seq, d)
```

```python
import functools

import jax
import jax.numpy as jnp
from jax import lax
from jax.experimental import pallas as pl
from jax.experimental.pallas import tpu as pltpu

N_META = 16
HEAD_DIM = 128
META_BLOCK = 128
EPS = 1e-6
LANES = 128
SB_KEYS = 256
FOX_DIAG_KEYS = 512
LOG2E = 1.4426950408889634
FFN_ROWS = 1024
PROJ_ROWS = 512
ATTN_ROWS = 1024
CAST_BLOCK_BYTES = 4 * 1024 * 1024
VMEM_LIMIT = 60 * 1024 * 1024

F32 = jnp.float32
BF16 = jnp.bfloat16
NEG_INF = float("-inf")


def _dot(a, b):
    return jnp.dot(a, b, preferred_element_type=F32)


def _dot_nt(a, b):
    return lax.dot_general(a, b, (((1,), (1,)), ((), ())), preferred_element_type=F32)


def _rms(x, g):
    return x * lax.rsqrt(jnp.mean(x * x, axis=-1, keepdims=True) + EPS) * g


def _head_norm(o, g):
    return o * lax.rsqrt(jnp.mean(o * o, axis=-1, keepdims=True) + EPS) * g


def _row_tile(rows, want):
    tile = min(rows, want)
    assert rows % tile == 0
    return tile


def _cast_kernel(x_ref, o_ref):
    o_ref[...] = x_ref[...].astype(BF16)


def _largest_divisor(n, step, limit):
    best = None
    for cand in range(step, min(n, limit) + 1, step):
        if n % cand == 0:
            best = cand
    return best


def _to_bf16(w, first_layer_only=False):
    depth, r, c = w.shape
    depth = 1 if first_layer_only else depth
    br = _largest_divisor(r, 8, CAST_BLOCK_BYTES // (4 * c))
    if br is not None and 4 * br * c * 4 >= CAST_BLOCK_BYTES:
        block = (None, br, c)
    else:
        block = (None, r, _largest_divisor(c, LANES, max(LANES, CAST_BLOCK_BYTES // (4 * r))))
    spec = pl.BlockSpec(block, lambda l, i, j: (l, i, j))
    return pl.pallas_call(
        _cast_kernel,
        out_shape=jax.ShapeDtypeStruct((depth, r, c), BF16),
        grid=(depth, r // block[1], c // block[2]),
        in_specs=[spec],
        out_specs=spec,
        compiler_params=pltpu.CompilerParams(
            dimension_semantics=("arbitrary",) * 3, vmem_limit_bytes=VMEM_LIMIT),
        name="cast_bf16",
    )(w)


def _ffn_kernel(h_ref, g_ref, wg_ref, wu_ref, wd_ref, *rest, final, cast_blocks):
    n_cast = len(cast_blocks)
    if final:
        gf_ref, *rest = rest
    cast_src, o_ref, cast_dst, xn_ref = rest[:n_cast], rest[n_cast], rest[n_cast + 1:-1], rest[-1]
    j = pl.program_id(1)
    step = pl.program_id(0) * pl.num_programs(1) + j
    for src_ref, dst_ref, n_blk in zip(cast_src, cast_dst, cast_blocks):
        @pl.when(step < n_blk)
        def _(src_ref=src_ref, dst_ref=dst_ref):
            dst_ref[...] = src_ref[...].astype(BF16)

    @pl.when(j == 0)
    def _():
        h = h_ref[...]
        xn_ref[...] = _rms(h, g_ref[...]).astype(BF16)
        o_ref[...] = h

    xn = xn_ref[...]
    gate = _dot(xn, wg_ref[...])
    up = _dot(xn, wu_ref[...])
    a = (gate * jax.nn.sigmoid(gate) * up * 0.5).astype(BF16)
    o_ref[...] += _dot(a, wd_ref[...])

    if final:
        @pl.when(j == pl.num_programs(1) - 1)
        def _():
            o_ref[...] = _rms(o_ref[...], gf_ref[...])


def _ffn(h, g, wg, wu, wd, *, layer, tf, final_gain=None, cast_along=()):
    rows, d = h.shape
    f = wg.shape[2]
    tm = _row_tile(rows, FFN_ROWS)
    final = final_gain is not None
    n_steps = (rows // tm) * (f // tf)
    n_j = f // tf
    cast_in, cast_in_specs, cast_out_specs, cast_shapes, cast_blocks = [], [], [], [], []
    for w, sel in cast_along:
        r, c = (w.shape[0] * w.shape[1] if sel is None else w.shape[1]), w.shape[2]
        rb = next(cand for cand in range(16, r + 1, 16) if r % cand == 0 and r // cand <= n_steps)
        n_blk = r // rb
        cast_blocks.append(n_blk)
        blk = lambda i, j, n_blk=n_blk: jnp.minimum(i * n_j + j, n_blk - 1)
        if sel is None:
            cast_in.append(w.reshape(r, c))
            cast_in_specs.append(pl.BlockSpec((rb, c), lambda i, j, blk=blk: (blk(i, j), 0)))
        else:
            cast_in.append(w)
            cast_in_specs.append(pl.BlockSpec((None, rb, c), lambda i, j, blk=blk, sel=sel: (sel, blk(i, j), 0)))
        cast_out_specs.append(pl.BlockSpec((rb, c), lambda i, j, blk=blk: (blk(i, j), 0)))
        cast_shapes.append(jax.ShapeDtypeStruct((r, c), BF16))
    in_specs = [
        pl.BlockSpec((tm, d), lambda i, j: (i, 0)),
        pl.BlockSpec((1, d), lambda i, j: (0, 0)),
        pl.BlockSpec((None, d, tf), lambda i, j: (layer, 0, j)),
        pl.BlockSpec((None, d, tf), lambda i, j: (layer, 0, j)),
        pl.BlockSpec((None, tf, d), lambda i, j: (layer, j, 0)),
    ]
    args = [h, g, wg, wu, wd]
    if final:
        in_specs.append(pl.BlockSpec((1, d), lambda i, j: (0, 0)))
        args.append(final_gain)
    out, *cast_out = pl.pallas_call(
        functools.partial(_ffn_kernel, final=final, cast_blocks=tuple(cast_blocks)),
        out_shape=(jax.ShapeDtypeStruct((rows, d), F32), *cast_shapes),
        grid=(rows // tm, f // tf),
        in_specs=in_specs + cast_in_specs,
        out_specs=(pl.BlockSpec((tm, d), lambda i, j: (i, 0)), *cast_out_specs),
        scratch_shapes=[pltpu.VMEM((tm, d), BF16)],
        compiler_params=pltpu.CompilerParams(
            dimension_semantics=("arbitrary", "arbitrary"), vmem_limit_bytes=VMEM_LIMIT),
        name="ffn_final" if final else "ffn",
    )(*args, *cast_in)
    return (out, cast_out) if cast_along else out


def _proj_kernel(h_ref, g_ref, w_ref, qkv_ref, fl_ref, xn_ref, *, n_heads, q_scale):
    hd = n_heads * HEAD_DIM
    xn_ref[...] = _rms(h_ref[...], g_ref[...]).astype(BF16)
    fl_ref[...] = _dot_nt(w_ref[6 * hd:, :], xn_ref[...])
    for j in range(6):
        acc = _dot_nt(xn_ref[...], w_ref[j * hd:(j + 1) * hd, :])
        if j in (0, 3):
            acc = acc * q_scale
        for hh in range(n_heads):
            qkv_ref[j * n_heads + hh] = acc[:, hh * HEAD_DIM:(hh + 1) * HEAD_DIM].astype(BF16)


def _proj(h, g, w_t, *, layer, n_heads):
    rows, d = h.shape
    tm = _row_tile(rows, PROJ_ROWS)
    return pl.pallas_call(
        functools.partial(_proj_kernel, n_heads=n_heads, q_scale=LOG2E * HEAD_DIM ** -0.5),
        out_shape=(jax.ShapeDtypeStruct((6 * n_heads, rows, HEAD_DIM), BF16),
                   jax.ShapeDtypeStruct((n_heads, rows), F32)),
        grid=(rows // tm,),
        in_specs=[
            pl.BlockSpec((tm, d), lambda i: (i, 0)),
            pl.BlockSpec((1, d), lambda i: (0, 0)),
            pl.BlockSpec((None,) + w_t.shape[1:], lambda i: (layer, 0, 0), pipeline_mode=pl.Buffered(1)),
        ],
        out_specs=(pl.BlockSpec((6 * n_heads, tm, HEAD_DIM), lambda i: (0, i, 0)),
                   pl.BlockSpec((n_heads, tm), lambda i: (0, i))),
        scratch_shapes=[pltpu.VMEM((tm, d), BF16)],
        compiler_params=pltpu.CompilerParams(
            dimension_semantics=("arbitrary",), vmem_limit_bytes=VMEM_LIMIT),
        name="in_proj",
    )(h, g, w_t)


def _scan_kernel(fl_ref, fll_ref, b_ref, nc_ref, ncl_ref, *, batch, seq):
    n_heads = fl_ref.shape[0]
    lane = lax.broadcasted_iota(jnp.int32, (n_heads, LANES), 1)
    bias = b_ref[...]

    def log_f(v):
        v = v + bias
        return jnp.minimum(v, 0.0) - jnp.log(1.0 + jnp.exp(-jnp.abs(v)))

    def scan_lanes(v):
        shift = 1
        while shift < LANES:
            v = v + jnp.where(lane >= shift, pltpu.roll(v, shift, axis=1), 0.0)
            shift *= 2
        return v

    c_lead = scan_lanes(jnp.where(lane >= META_BLOCK - N_META, log_f(fll_ref[...]), 0.0))
    ncl_ref[...] = -LOG2E * c_lead

    def body(k, carries):
        out = []
        for b in range(batch):
            start = pl.multiple_of(b * seq + k * LANES, LANES)
            c = scan_lanes(log_f(fl_ref[:, pl.ds(start, LANES)])) + carries[b]
            nc_ref[:, pl.ds(start, LANES)] = -LOG2E * c
            out.append(c[:, LANES - 1:LANES])
        return tuple(out)

    lax.fori_loop(0, seq // LANES, body, (c_lead[:, LANES - 1:LANES],) * batch)


def _scan(fl, fl_lead, b_forget, *, batch, seq):
    n_heads = fl.shape[0]
    return pl.pallas_call(
        functools.partial(_scan_kernel, batch=batch, seq=seq),
        out_shape=(jax.ShapeDtypeStruct(fl.shape, F32), jax.ShapeDtypeStruct(fl_lead.shape, F32)),
        name="forget_scan",
    )(fl, fl_lead, b_forget.reshape(n_heads, 1))


def _fox_step(q, k, v, nc, carry, mask):
    m, l, acc = carry
    s = _dot_nt(q, k) + nc
    if mask is not None:
        s = jnp.where(mask, s, NEG_INF)
    m_new = jnp.maximum(m, jnp.max(s, axis=-1, keepdims=True))
    alpha = jnp.exp2(m - m_new)
    p = jnp.exp2(s - m_new)
    l = alpha * l + jnp.sum(p, axis=-1, keepdims=True)
    acc = alpha * acc + _dot(p.astype(BF16), v)
    return m_new, l, acc


def _sb_weights(q, k, mask, suffix):
    z = _dot_nt(q, k)
    nz = -z
    l1m = jnp.minimum(nz, 0.0) - jnp.log(1.0 + jnp.exp2(jnp.minimum(z, nz))) * LOG2E
    if mask is not None:
        l1m = jnp.where(mask, l1m, 0.0)
    incl = _dot(l1m.astype(BF16), suffix)
    e = z + incl
    if mask is not None:
        e = jnp.where(mask, e, NEG_INF)
    return jnp.exp2(e).astype(BF16), incl[:, :1]


def _sb_block(q, k, v, r, acc, mask, suffix):
    a, total = _sb_weights(q, k, mask, suffix)
    return r + total, acc + jnp.exp2(r) * _dot(a, v)


def _mixer_kernel(qf_ref, kf_ref, vf_ref, qs_ref, ks_ref, vs_ref, kfm_ref, vfm_ref, ksm_ref, vsm_ref,
                  ncx_ref, ncm_ref, gf_ref, gs_ref, *rest, tq, seq):
    n_cast = (len(rest) - 2) // 2
    of_ref, os_ref = rest[n_cast:n_cast + 2]
    for src_ref, dst_ref in zip(rest[:n_cast], rest[n_cast + 2:]):
        dst_ref[...] = src_ref[...].astype(BF16)

    n_sub = tq // SB_KEYS
    fox_keys = min(tq, FOX_DIAG_KEYS)
    row = lax.broadcasted_iota(jnp.int32, (tq, SB_KEYS), 0)
    col = lax.broadcasted_iota(jnp.int32, (tq, SB_KEYS), 1)
    strict = col < row
    causal = (lax.broadcasted_iota(jnp.int32, (tq, fox_keys), 1)
              <= lax.broadcasted_iota(jnp.int32, (tq, fox_keys), 0))
    urow = lax.broadcasted_iota(jnp.int32, (SB_KEYS, SB_KEYS), 0)
    ucol = lax.broadcasted_iota(jnp.int32, (SB_KEYS, SB_KEYS), 1)
    suffix = (urow >= ucol).astype(BF16)
    mrow = lax.broadcasted_iota(jnp.int32, (META_BLOCK, META_BLOCK), 0)
    mcol = lax.broadcasted_iota(jnp.int32, (META_BLOCK, META_BLOCK), 1)
    suffix_m = (mrow >= mcol).astype(BF16)
    meta_valid = lax.broadcasted_iota(jnp.int32, (1, META_BLOCK), 1) >= META_BLOCK - N_META
    meta_bias = jnp.where(meta_valid, ncm_ref[...], NEG_INF)

    def q_tile(t, _):
        q0 = pl.multiple_of(t * tq, tq)
        qf = qf_ref[pl.ds(q0, tq), :]
        qs = qs_ref[pl.ds(q0, tq), :]

        s = _dot_nt(qf, kfm_ref[...]) + meta_bias
        m = jnp.max(s, axis=-1, keepdims=True)
        p = jnp.exp2(s - m)
        fox = (m, jnp.sum(p, axis=-1, keepdims=True), _dot(p.astype(BF16), vfm_ref[...]))

        sb = None
        for d in reversed(range(n_sub)):
            r0 = d * SB_KEYS
            k0 = pl.multiple_of(q0 + r0, SB_KEYS)
            if sb is None:
                r, acc = jnp.zeros((tq - r0, 1), F32), jnp.zeros((tq - r0, HEAD_DIM), F32)
            else:
                r = jnp.concatenate([jnp.zeros((SB_KEYS, 1), F32), sb[0]], axis=0)
                acc = jnp.concatenate([jnp.zeros((SB_KEYS, HEAD_DIM), F32), sb[1]], axis=0)
            sb = _sb_block(qs[r0:], ks_ref[pl.ds(k0, SB_KEYS), :], vs_ref[pl.ds(k0, SB_KEYS), :],
                           r, acc, strict[:tq - r0], suffix)

        def kv_step(jj, carry):
            fox, sb = carry
            for d in range(tq // fox_keys):
                k0 = pl.multiple_of(jj * tq + d * fox_keys, fox_keys)
                fox = _fox_step(qf, kf_ref[pl.ds(k0, fox_keys), :], vf_ref[pl.ds(k0, fox_keys), :],
                                ncx_ref[:, pl.ds(k0, fox_keys)], fox, None)
            for d in reversed(range(n_sub)):
                k0 = pl.multiple_of((t - 1 - jj) * tq + d * SB_KEYS, SB_KEYS)
                sb = _sb_block(qs, ks_ref[pl.ds(k0, SB_KEYS), :], vs_ref[pl.ds(k0, SB_KEYS), :],
                               sb[0], sb[1], None, suffix)
            return fox, sb

        fox, sb = lax.fori_loop(0, t, kv_step, (fox, sb))

        for d in range(tq // fox_keys):
            r0 = d * fox_keys
            k0 = pl.multiple_of(q0 + r0, fox_keys)
            sub = _fox_step(qf[r0:], kf_ref[pl.ds(k0, fox_keys), :], vf_ref[pl.ds(k0, fox_keys), :],
                            ncx_ref[:, pl.ds(k0, fox_keys)], tuple(a[r0:] for a in fox), causal[:tq - r0])
            fox = tuple(jnp.concatenate([a[:r0], b], axis=0) for a, b in zip(fox, sub)) if r0 else sub
        sb = _sb_block(qs, ksm_ref[...], vsm_ref[...], sb[0], sb[1], meta_valid, suffix_m)

        of_ref[pl.ds(q0, tq), :] = _head_norm(fox[2] / fox[1], gf_ref[...]).astype(BF16)
        os_ref[pl.ds(q0, tq), :] = _head_norm(sb[1], gs_ref[...]).astype(BF16)
        return 0

    lax.fori_loop(0, seq // tq, q_tile, 0)


def _attention(qkv, qkv_lead, nc, nc_lead, g_fox, g_sb, *, batch, seq, n_heads, cast_along=()):
    H = n_heads
    hd = H * HEAD_DIM
    tq = _row_tile(seq, ATTN_ROWS)
    assert tq % SB_KEYS == 0 and tq % min(tq, FOX_DIAG_KEYS) == 0
    cast_in, cast_specs, cast_shapes = [], [], []
    for w in cast_along:
        rows, c = w.shape[0] * w.shape[1], w.shape[2]
        rb = rows // (batch * H)
        assert rb % 16 == 0 and rb * batch * H == rows
        cast_in.append(w.reshape(rows, c))
        cast_specs.append(pl.BlockSpec((rb, c), lambda b, h: (b * H + h, 0)))
        cast_shapes.append(jax.ShapeDtypeStruct((rows, c), BF16))

    def tok(base):
        return pl.BlockSpec((None, seq, HEAD_DIM), lambda b, h: (base + h, b, 0))

    def meta(base):
        return pl.BlockSpec((None, META_BLOCK, HEAD_DIM), lambda b, h: (base + h, 0, 0))

    gain = pl.BlockSpec((1, HEAD_DIM), lambda b, h: (0, h))
    out_spec = pl.BlockSpec((seq, HEAD_DIM), lambda b, h: (b, h))
    out_shape = jax.ShapeDtypeStruct((batch * seq, hd), BF16)
    o_f, o_s, *cast_out = pl.pallas_call(
        functools.partial(_mixer_kernel, tq=tq, seq=seq),
        out_shape=(out_shape, out_shape, *cast_shapes),
        grid=(batch, H),
        in_specs=[tok(0), tok(H), tok(2 * H), tok(3 * H), tok(4 * H), tok(5 * H),
                  meta(H), meta(2 * H), meta(4 * H), meta(5 * H),
                  pl.BlockSpec((None, 1, seq), lambda b, h: (h, 0, b)),
                  pl.BlockSpec((None, 1, META_BLOCK), lambda b, h: (h, 0, 0)),
                  gain, gain, *cast_specs],
        out_specs=(out_spec, out_spec, *cast_specs),
        compiler_params=pltpu.CompilerParams(
            dimension_semantics=("arbitrary", "arbitrary"), vmem_limit_bytes=VMEM_LIMIT),
        name="mixer_attention",
    )(qkv, qkv, qkv, qkv, qkv, qkv, qkv_lead, qkv_lead, qkv_lead, qkv_lead,
      nc.reshape(H, 1, -1), nc_lead.reshape(H, 1, -1), g_fox, g_sb, *cast_in)
    return o_f, o_s, [o.reshape(w.shape) for o, w in zip(cast_out, cast_along)]


def _meta_attn_kernel(qkv_ref, ncm_ref, g_ref, o_ref, *, n_heads):
    H = n_heads
    row = lax.broadcasted_iota(jnp.int32, (META_BLOCK, META_BLOCK), 0)
    col = lax.broadcasted_iota(jnp.int32, (META_BLOCK, META_BLOCK), 1)
    valid = col >= META_BLOCK - N_META
    fox_allowed = (col <= row) & (valid | (col == row))
    sb_allowed = (col < row) & valid
    suffix = (row >= col).astype(BF16)
    for hh in range(H):
        s = jnp.where(fox_allowed, _dot_nt(qkv_ref[hh], qkv_ref[H + hh]) + ncm_ref[hh], NEG_INF)
        p = jnp.exp2(s - jnp.max(s, axis=-1, keepdims=True))
        o = _dot(p.astype(BF16), qkv_ref[2 * H + hh]) / jnp.sum(p, axis=-1, keepdims=True)
        cols = slice(hh * HEAD_DIM, (hh + 1) * HEAD_DIM)
        o_ref[:, cols] = _head_norm(o, g_ref[:, cols]).astype(BF16)
    for hh in range(H):
        a, _ = _sb_weights(qkv_ref[3 * H + hh], qkv_ref[4 * H + hh], sb_allowed, suffix)
        cols = slice((H + hh) * HEAD_DIM, (H + hh + 1) * HEAD_DIM)
        o_ref[:, cols] = _head_norm(_dot(a, qkv_ref[5 * H + hh]), g_ref[:, cols]).astype(BF16)


def _meta_attention(qkv_lead, nc_lead, g_cat, *, n_heads):
    H = n_heads
    return pl.pallas_call(
        functools.partial(_meta_attn_kernel, n_heads=H),
        out_shape=jax.ShapeDtypeStruct((META_BLOCK, 2 * H * HEAD_DIM), BF16),
        name="meta_attention",
    )(qkv_lead, nc_lead.reshape(H, 1, -1), g_cat)


def _out_proj_kernel(h_ref, *rest, widths):
    *o_refs, w_ref, out_ref = rest
    acc = h_ref[...]
    start = 0
    for o_ref, width in zip(o_refs, widths):
        acc = acc + _dot(o_ref[...], w_ref[start:start + width, :])
        start += width
    out_ref[...] = acc


def _out_proj(h, parts, w, *, layer):
    rows, d = h.shape
    tm = _row_tile(rows, PROJ_ROWS)
    widths = tuple(p.shape[1] for p in parts)
    assert sum(widths) == w.shape[1]
    return pl.pallas_call(
        functools.partial(_out_proj_kernel, widths=widths),
        out_shape=jax.ShapeDtypeStruct((rows, d), F32),
        grid=(rows // tm,),
        in_specs=[pl.BlockSpec((tm, d), lambda i: (i, 0))]
        + [pl.BlockSpec((tm, width), lambda i: (i, 0)) for width in widths]
        + [pl.BlockSpec((None,) + w.shape[1:], lambda i: (layer, 0, 0))],
        out_specs=pl.BlockSpec((tm, d), lambda i: (i, 0)),
        compiler_params=pltpu.CompilerParams(
            dimension_semantics=("arbitrary",), vmem_limit_bytes=VMEM_LIMIT),
        name="out_proj",
    )(h, *parts, w)


def kernel(x, meta_tokens, ffn1_norm, ffn1_w_gate, ffn1_w_up, ffn1_w_down, mix_norm, w_in, b_forget, g_fox, g_sb, w_out, ffn2_norm, ffn2_w_gate, ffn2_w_up, ffn2_w_down, final_norm):
    batch, seq, d = x.shape
    depth = w_in.shape[0]
    n_heads = g_fox.shape[1] // HEAD_DIM
    hd = n_heads * HEAD_DIM
    f = ffn1_w_gate.shape[2]
    assert meta_tokens.shape[0] == N_META and w_in.shape[2] == 6 * hd + n_heads
    tf = 512 if f % 512 == 0 else LANES

    tok = x.reshape(batch * seq, d)
    lead = jnp.concatenate([jnp.zeros((META_BLOCK - N_META, d), x.dtype), meta_tokens.astype(x.dtype)], axis=0)

    row = lambda v: v.reshape(1, -1)
    ffn1_w = (ffn1_w_gate, ffn1_w_up, ffn1_w_down)
    w1 = [[_to_bf16(w, first_layer_only=True) for w in ffn1_w]]
    w_in_t = _to_bf16(jnp.swapaxes(w_in, 1, 2))
    for l in range(depth):
        last = l == depth - 1
        if l == 0:
            along = [(w, k) for k in range(1, depth) for w in ffn1_w] + [(w_out, None)]
            tok, cast = _ffn(tok, row(ffn1_norm[l]), *w1[0], layer=0, tf=tf, cast_along=along)
            w1 += [[c[None] for c in cast[3 * (k - 1):3 * k]] for k in range(1, depth)]
            w_o = cast[-1].reshape(w_out.shape)
        else:
            tok = _ffn(tok, row(ffn1_norm[l]), *w1[l], layer=0, tf=tf)
        lead = _ffn(lead, row(ffn1_norm[l]), *w1[l], layer=0, tf=tf)

        qkv, fl = _proj(tok, row(mix_norm[l]), w_in_t, layer=l, n_heads=n_heads)
        qkv_lead, fl_lead = _proj(lead, row(mix_norm[l]), w_in_t, layer=l, n_heads=n_heads)
        nc, nc_lead = _scan(fl, fl_lead, b_forget[l], batch=batch, seq=seq)
        o_f, o_s, cast = _attention(qkv, qkv_lead, nc, nc_lead, row(g_fox[l]), row(g_sb[l]),
                                    batch=batch, seq=seq, n_heads=n_heads,
                                    cast_along=(ffn2_w_gate, ffn2_w_up, ffn2_w_down) if l == 0 else ())
        if l == 0:
            w2g, w2u, w2d = cast
        tok = _out_proj(tok, [o_f, o_s], w_o, layer=l)
        if not last:
            o_m = _meta_attention(qkv_lead, nc_lead, row(jnp.concatenate([g_fox[l], g_sb[l]])), n_heads=n_heads)
            lead = _out_proj(lead, [o_m], w_o, layer=l)
            lead = _ffn(lead, row(ffn2_norm[l]), w2g, w2u, w2d, layer=l, tf=tf)

        tok = _ffn(tok, row(ffn2_norm[l]), w2g, w2u, w2d, layer=l, tf=tf,
                   final_gain=row(final_norm) if last else None)
    return tok.reshape(batch, seq, d)
```

```python
import functools

import jax
import jax.numpy as jnp
from jax import lax
from jax.experimental import pallas as pl
from jax.experimental.pallas import tpu as pltpu

N_META = 16
HEAD_DIM = 128
META_BLOCK = 128
EPS = 1e-6
LANES = 128
SB_KEYS = 256
FOX_DIAG_KEYS = 512
LOG2E = 1.4426950408889634
FFN_ROWS = 1024
PROJ_ROWS = 512
ATTN_ROWS = 1024
CAST_BLOCK_BYTES = 4 * 1024 * 1024
VMEM_LIMIT = 60 * 1024 * 1024

F32 = jnp.float32
BF16 = jnp.bfloat16
NEG_INF = float("-inf")


def _dot(a, b):
    return jnp.dot(a, b, preferred_element_type=F32)


def _dot_nt(a, b):
    return lax.dot_general(a, b, (((1,), (1,)), ((), ())), preferred_element_type=F32)


def _rms(x, g):
    return x * lax.rsqrt(jnp.mean(x * x, axis=-1, keepdims=True) + EPS) * g


def _head_norm(o, g):
    return o * lax.rsqrt(jnp.mean(o * o, axis=-1, keepdims=True) + EPS) * g


def _row_tile(rows, want):
    tile = min(rows, want)
    assert rows % tile == 0
    return tile


def _cast_kernel(x_ref, o_ref):
    o_ref[...] = x_ref[...].astype(BF16)


def _largest_divisor(n, step, limit):
    best = None
    for cand in range(step, min(n, limit) + 1, step):
        if n % cand == 0:
            best = cand
    return best


def _to_bf16(w, first_layer_only=False):
    depth, r, c = w.shape
    depth = 1 if first_layer_only else depth
    br = _largest_divisor(r, 8, CAST_BLOCK_BYTES // (4 * c))
    if br is not None and 4 * br * c * 4 >= CAST_BLOCK_BYTES:
        block = (None, br, c)
    else:
        block = (None, r, _largest_divisor(c, LANES, max(LANES, CAST_BLOCK_BYTES // (4 * r))))
    spec = pl.BlockSpec(block, lambda l, i, j: (l, i, j))
    return pl.pallas_call(
        _cast_kernel,
        out_shape=jax.ShapeDtypeStruct((depth, r, c), BF16),
        grid=(depth, r // block[1], c // block[2]),
        in_specs=[spec],
        out_specs=spec,
        compiler_params=pltpu.CompilerParams(
            dimension_semantics=("arbitrary",) * 3, vmem_limit_bytes=VMEM_LIMIT),
        name="cast_bf16",
    )(w)


def _swiglu_half(xn, wg_ref, wu_ref, wd_ref):
    gate = _dot(xn, wg_ref[...])
    up = _dot(xn, wu_ref[...])
    return _dot((gate * jax.nn.sigmoid(gate) * up * 0.5).astype(BF16), wd_ref[...])


def _ffn_kernel(h_ref, g_ref, wg_ref, wu_ref, wd_ref, *rest, final, cast_blocks, with_lead):
    n_cast = len(cast_blocks)
    if final:
        gf_ref, *rest = rest
    cast_src, rest = rest[:n_cast], rest[n_cast:]
    if with_lead:
        lead_ref, *rest = rest
    o_ref, cast_dst, rest = rest[0], rest[1:1 + n_cast], rest[1 + n_cast:]
    if with_lead:
        lead_o_ref, xn_ref, xnl_ref = rest
    else:
        (xn_ref,) = rest
    j = pl.program_id(1)
    step = pl.program_id(0) * pl.num_programs(1) + j
    for src_ref, dst_ref, n_blk in zip(cast_src, cast_dst, cast_blocks):
        @pl.when(step < n_blk)
        def _(src_ref=src_ref, dst_ref=dst_ref):
            dst_ref[...] = src_ref[...].astype(BF16)

    @pl.when(j == 0)
    def _():
        h = h_ref[...]
        xn_ref[...] = _rms(h, g_ref[...]).astype(BF16)
        o_ref[...] = h

    o_ref[...] += _swiglu_half(xn_ref[...], wg_ref, wu_ref, wd_ref)

    if with_lead:
        @pl.when((pl.program_id(0) == 0) & (j == 0))
        def _():
            hl = lead_ref[...]
            xnl_ref[...] = _rms(hl, g_ref[...]).astype(BF16)
            lead_o_ref[...] = hl

        @pl.when(pl.program_id(0) == 0)
        def _():
            lead_o_ref[...] += _swiglu_half(xnl_ref[...], wg_ref, wu_ref, wd_ref)

    if final:
        @pl.when(j == pl.num_programs(1) - 1)
        def _():
            o_ref[...] = _rms(o_ref[...], gf_ref[...])


def _ffn(h, g, wg, wu, wd, *, layer, tf, final_gain=None, cast_along=(), lead=None):
    rows, d = h.shape
    f = wg.shape[2]
    tm = _row_tile(rows, FFN_ROWS)
    final = final_gain is not None
    n_steps = (rows // tm) * (f // tf)
    n_j = f // tf
    cast_in, cast_in_specs, cast_out_specs, cast_shapes, cast_blocks = [], [], [], [], []
    for w, sel in cast_along:
        r, c = (w.shape[0] * w.shape[1] if sel is None else w.shape[1]), w.shape[2]
        rb = next(cand for cand in range(16, r + 1, 16) if r % cand == 0 and r // cand <= n_steps)
        n_blk = r // rb
        cast_blocks.append(n_blk)
        blk = lambda i, j, n_blk=n_blk: jnp.minimum(i * n_j + j, n_blk - 1)
        if sel is None:
            cast_in.append(w.reshape(r, c))
            cast_in_specs.append(pl.BlockSpec((rb, c), lambda i, j, blk=blk: (blk(i, j), 0)))
        else:
            cast_in.append(w)
            cast_in_specs.append(pl.BlockSpec((None, rb, c), lambda i, j, blk=blk, sel=sel: (sel, blk(i, j), 0)))
        cast_out_specs.append(pl.BlockSpec((rb, c), lambda i, j, blk=blk: (blk(i, j), 0)))
        cast_shapes.append(jax.ShapeDtypeStruct((r, c), BF16))
    in_specs = [
        pl.BlockSpec((tm, d), lambda i, j: (i, 0)),
        pl.BlockSpec((1, d), lambda i, j: (0, 0)),
        pl.BlockSpec((None, d, tf), lambda i, j: (layer, 0, j)),
        pl.BlockSpec((None, d, tf), lambda i, j: (layer, 0, j)),
        pl.BlockSpec((None, tf, d), lambda i, j: (layer, j, 0)),
    ]
    args = [h, g, wg, wu, wd]
    if final:
        in_specs.append(pl.BlockSpec((1, d), lambda i, j: (0, 0)))
        args.append(final_gain)
    lead_spec = [] if lead is None else [pl.BlockSpec(lead.shape, lambda i, j: (0, 0))]
    lead_shape = [] if lead is None else [jax.ShapeDtypeStruct(lead.shape, F32)]
    lead_scratch = [] if lead is None else [pltpu.VMEM(lead.shape, BF16)]
    out, *more = pl.pallas_call(
        functools.partial(_ffn_kernel, final=final, cast_blocks=tuple(cast_blocks), with_lead=lead is not None),
        out_shape=(jax.ShapeDtypeStruct((rows, d), F32), *cast_shapes, *lead_shape),
        grid=(rows // tm, f // tf),
        in_specs=in_specs + cast_in_specs + lead_spec,
        out_specs=(pl.BlockSpec((tm, d), lambda i, j: (i, 0)), *cast_out_specs, *lead_spec),
        scratch_shapes=[pltpu.VMEM((tm, d), BF16)] + lead_scratch,
        compiler_params=pltpu.CompilerParams(
            dimension_semantics=("arbitrary", "arbitrary"), vmem_limit_bytes=VMEM_LIMIT),
        name="ffn_final" if final else "ffn",
    )(*args, *cast_in, *([] if lead is None else [lead]))
    return out, more[:len(cast_in)], (None if lead is None else more[-1])


def _proj_kernel(h_ref, g_ref, w_ref, qkv_ref, fl_ref, xn_ref, *, n_heads, q_scale):
    hd = n_heads * HEAD_DIM
    xn_ref[...] = _rms(h_ref[...], g_ref[...]).astype(BF16)
    fl_ref[...] = _dot_nt(w_ref[6 * hd:, :], xn_ref[...])
    for j in range(6):
        acc = _dot_nt(xn_ref[...], w_ref[j * hd:(j + 1) * hd, :])
        if j in (0, 3):
            acc = acc * q_scale
        for hh in range(n_heads):
            qkv_ref[j * n_heads + hh] = acc[:, hh * HEAD_DIM:(hh + 1) * HEAD_DIM].astype(BF16)


def _proj(h, g, w_t, *, layer, n_heads):
    rows, d = h.shape
    tm = _row_tile(rows, PROJ_ROWS)
    return pl.pallas_call(
        functools.partial(_proj_kernel, n_heads=n_heads, q_scale=LOG2E * HEAD_DIM ** -0.5),
        out_shape=(jax.ShapeDtypeStruct((6 * n_heads, rows, HEAD_DIM), BF16),
                   jax.ShapeDtypeStruct((n_heads, rows), F32)),
        grid=(rows // tm,),
        in_specs=[
            pl.BlockSpec((tm, d), lambda i: (i, 0)),
            pl.BlockSpec((1, d), lambda i: (0, 0)),
            pl.BlockSpec((None,) + w_t.shape[1:], lambda i: (layer, 0, 0), pipeline_mode=pl.Buffered(1)),
        ],
        out_specs=(pl.BlockSpec((6 * n_heads, tm, HEAD_DIM), lambda i: (0, i, 0)),
                   pl.BlockSpec((n_heads, tm), lambda i: (0, i))),
        scratch_shapes=[pltpu.VMEM((tm, d), BF16)],
        compiler_params=pltpu.CompilerParams(
            dimension_semantics=("arbitrary",), vmem_limit_bytes=VMEM_LIMIT),
        name="in_proj",
    )(h, g, w_t)


def _scan_kernel(fl_ref, fll_ref, b_ref, nc_ref, ncl_ref, *, batch, seq):
    n_heads = fl_ref.shape[0]
    lane = lax.broadcasted_iota(jnp.int32, (n_heads, LANES), 1)
    bias = b_ref[...]

    def log_f(v):
        v = v + bias
        return jnp.minimum(v, 0.0) - jnp.log(1.0 + jnp.exp(-jnp.abs(v)))

    def scan_lanes(v):
        shift = 1
        while shift < LANES:
            v = v + jnp.where(lane >= shift, pltpu.roll(v, shift, axis=1), 0.0)
            shift *= 2
        return v

    c_lead = scan_lanes(jnp.where(lane >= META_BLOCK - N_META, log_f(fll_ref[...]), 0.0))
    ncl_ref[...] = -LOG2E * c_lead

    def body(k, carries):
        out = []
        for b in range(batch):
            start = pl.multiple_of(b * seq + k * LANES, LANES)
            c = scan_lanes(log_f(fl_ref[:, pl.ds(start, LANES)])) + carries[b]
            nc_ref[:, pl.ds(start, LANES)] = -LOG2E * c
            out.append(c[:, LANES - 1:LANES])
        return tuple(out)

    lax.fori_loop(0, seq // LANES, body, (c_lead[:, LANES - 1:LANES],) * batch)


def _scan(fl, fl_lead, b_forget, *, batch, seq):
    n_heads = fl.shape[0]
    return pl.pallas_call(
        functools.partial(_scan_kernel, batch=batch, seq=seq),
        out_shape=(jax.ShapeDtypeStruct(fl.shape, F32), jax.ShapeDtypeStruct(fl_lead.shape, F32)),
        name="forget_scan",
    )(fl, fl_lead, b_forget.reshape(n_heads, 1))


def _fox_step(q, k, v, nc, carry, mask):
    m, l, acc = carry
    s = _dot_nt(q, k) + nc
    if mask is not None:
        s = jnp.where(mask, s, NEG_INF)
    m_new = jnp.maximum(m, jnp.max(s, axis=-1, keepdims=True))
    alpha = jnp.exp2(m - m_new)
    p = jnp.exp2(s - m_new)
    l = alpha * l + jnp.sum(p, axis=-1, keepdims=True)
    acc = alpha * acc + _dot(p.astype(BF16), v)
    return m_new, l, acc


def _sb_weights(q, k, mask, suffix):
    z = _dot_nt(q, k)
    nz = -z
    l1m = jnp.minimum(nz, 0.0) - jnp.log(1.0 + jnp.exp2(jnp.minimum(z, nz))) * LOG2E
    if mask is not None:
        l1m = jnp.where(mask, l1m, 0.0)
    incl = _dot(l1m.astype(BF16), suffix)
    e = z + incl
    if mask is not None:
        e = jnp.where(mask, e, NEG_INF)
    return jnp.exp2(e).astype(BF16), incl[:, :1]


def _sb_block(q, k, v, r, acc, mask, suffix):
    a, total = _sb_weights(q, k, mask, suffix)
    return r + total, acc + jnp.exp2(r) * _dot(a, v)


def _mixer_kernel(qf_ref, kf_ref, vf_ref, qs_ref, ks_ref, vs_ref, kfm_ref, vfm_ref, ksm_ref, vsm_ref,
                  ncx_ref, ncm_ref, gf_ref, gs_ref, *rest, tq, seq):
    n_cast = (len(rest) - 2) // 2
    of_ref, os_ref = rest[n_cast:n_cast + 2]
    for src_ref, dst_ref in zip(rest[:n_cast], rest[n_cast + 2:]):
        dst_ref[...] = src_ref[...].astype(BF16)

    n_sub = tq // SB_KEYS
    fox_keys = min(tq, FOX_DIAG_KEYS)
    row = lax.broadcasted_iota(jnp.int32, (tq, SB_KEYS), 0)
    col = lax.broadcasted_iota(jnp.int32, (tq, SB_KEYS), 1)
    strict = col < row
    causal = (lax.broadcasted_iota(jnp.int32, (tq, fox_keys), 1)
              <= lax.broadcasted_iota(jnp.int32, (tq, fox_keys), 0))
    urow = lax.broadcasted_iota(jnp.int32, (SB_KEYS, SB_KEYS), 0)
    ucol = lax.broadcasted_iota(jnp.int32, (SB_KEYS, SB_KEYS), 1)
    suffix = (urow >= ucol).astype(BF16)
    mrow = lax.broadcasted_iota(jnp.int32, (META_BLOCK, META_BLOCK), 0)
    mcol = lax.broadcasted_iota(jnp.int32, (META_BLOCK, META_BLOCK), 1)
    suffix_m = (mrow >= mcol).astype(BF16)
    meta_valid = lax.broadcasted_iota(jnp.int32, (1, META_BLOCK), 1) >= META_BLOCK - N_META
    meta_bias = jnp.where(meta_valid, ncm_ref[...], NEG_INF)

    def q_tile(t, _):
        q0 = pl.multiple_of(t * tq, tq)
        qf = qf_ref[pl.ds(q0, tq), :]
        qs = qs_ref[pl.ds(q0, tq), :]

        s = _dot_nt(qf, kfm_ref[...]) + meta_bias
        m = jnp.max(s, axis=-1, keepdims=True)
        p = jnp.exp2(s - m)
        fox = (m, jnp.sum(p, axis=-1, keepdims=True), _dot(p.astype(BF16), vfm_ref[...]))

        sb = None
        for d in reversed(range(n_sub)):
            r0 = d * SB_KEYS
            k0 = pl.multiple_of(q0 + r0, SB_KEYS)
            if sb is None:
                r, acc = jnp.zeros((tq - r0, 1), F32), jnp.zeros((tq - r0, HEAD_DIM), F32)
            else:
                r = jnp.concatenate([jnp.zeros((SB_KEYS, 1), F32), sb[0]], axis=0)
                acc = jnp.concatenate([jnp.zeros((SB_KEYS, HEAD_DIM), F32), sb[1]], axis=0)
            sb = _sb_block(qs[r0:], ks_ref[pl.ds(k0, SB_KEYS), :], vs_ref[pl.ds(k0, SB_KEYS), :],
                           r, acc, strict[:tq - r0], suffix)

        def kv_step(jj, carry):
            fox, sb = carry
            for d in range(tq // fox_keys):
                k0 = pl.multiple_of(jj * tq + d * fox_keys, fox_keys)
                fox = _fox_step(qf, kf_ref[pl.ds(k0, fox_keys), :], vf_ref[pl.ds(k0, fox_keys), :],
                                ncx_ref[:, pl.ds(k0, fox_keys)], fox, None)
            for d in reversed(range(n_sub)):
                k0 = pl.multiple_of((t - 1 - jj) * tq + d * SB_KEYS, SB_KEYS)
                sb = _sb_block(qs, ks_ref[pl.ds(k0, SB_KEYS), :], vs_ref[pl.ds(k0, SB_KEYS), :],
                               sb[0], sb[1], None, suffix)
            return fox, sb

        fox, sb = lax.fori_loop(0, t, kv_step, (fox, sb))

        for d in range(tq // fox_keys):
            r0 = d * fox_keys
            k0 = pl.multiple_of(q0 + r0, fox_keys)
            sub = _fox_step(qf[r0:], kf_ref[pl.ds(k0, fox_keys), :], vf_ref[pl.ds(k0, fox_keys), :],
                            ncx_ref[:, pl.ds(k0, fox_keys)], tuple(a[r0:] for a in fox), causal[:tq - r0])
            fox = tuple(jnp.concatenate([a[:r0], b], axis=0) for a, b in zip(fox, sub)) if r0 else sub
        sb = _sb_block(qs, ksm_ref[...], vsm_ref[...], sb[0], sb[1], meta_valid, suffix_m)

        of_ref[pl.ds(q0, tq), :] = _head_norm(fox[2] / fox[1], gf_ref[...]).astype(BF16)
        os_ref[pl.ds(q0, tq), :] = _head_norm(sb[1], gs_ref[...]).astype(BF16)
        return 0

    lax.fori_loop(0, seq // tq, q_tile, 0)


def _attention(qkv, qkv_lead, nc, nc_lead, g_fox, g_sb, *, batch, seq, n_heads, cast_along=()):
    H = n_heads
    hd = H * HEAD_DIM
    tq = _row_tile(seq, ATTN_ROWS)
    assert tq % SB_KEYS == 0 and tq % min(tq, FOX_DIAG_KEYS) == 0
    cast_in, cast_specs, cast_shapes = [], [], []
    for w in cast_along:
        rows, c = w.shape[0] * w.shape[1], w.shape[2]
        rb = rows // (batch * H)
        assert rb % 16 == 0 and rb * batch * H == rows
        cast_in.append(w.reshape(rows, c))
        cast_specs.append(pl.BlockSpec((rb, c), lambda b, h: (b * H + h, 0)))
        cast_shapes.append(jax.ShapeDtypeStruct((rows, c), BF16))

    def tok(base):
        return pl.BlockSpec((None, seq, HEAD_DIM), lambda b, h: (base + h, b, 0))

    def meta(base):
        return pl.BlockSpec((None, META_BLOCK, HEAD_DIM), lambda b, h: (base + h, 0, 0))

    gain = pl.BlockSpec((1, HEAD_DIM), lambda b, h: (0, h))
    out_spec = pl.BlockSpec((seq, HEAD_DIM), lambda b, h: (b, h))
    out_shape = jax.ShapeDtypeStruct((batch * seq, hd), BF16)
    o_f, o_s, *cast_out = pl.pallas_call(
        functools.partial(_mixer_kernel, tq=tq, seq=seq),
        out_shape=(out_shape, out_shape, *cast_shapes),
        grid=(batch, H),
        in_specs=[tok(0), tok(H), tok(2 * H), tok(3 * H), tok(4 * H), tok(5 * H),
                  meta(H), meta(2 * H), meta(4 * H), meta(5 * H),
                  pl.BlockSpec((None, 1, seq), lambda b, h: (h, 0, b)),
                  pl.BlockSpec((None, 1, META_BLOCK), lambda b, h: (h, 0, 0)),
                  gain, gain, *cast_specs],
        out_specs=(out_spec, out_spec, *cast_specs),
        compiler_params=pltpu.CompilerParams(
            dimension_semantics=("arbitrary", "arbitrary"), vmem_limit_bytes=VMEM_LIMIT),
        name="mixer_attention",
    )(qkv, qkv, qkv, qkv, qkv, qkv, qkv_lead, qkv_lead, qkv_lead, qkv_lead,
      nc.reshape(H, 1, -1), nc_lead.reshape(H, 1, -1), g_fox, g_sb, *cast_in)
    return o_f, o_s, [o.reshape(w.shape) for o, w in zip(cast_out, cast_along)]


def _meta_attn_kernel(qkv_ref, ncm_ref, g_ref, o_ref, *, n_heads):
    H = n_heads
    row = lax.broadcasted_iota(jnp.int32, (META_BLOCK, META_BLOCK), 0)
    col = lax.broadcasted_iota(jnp.int32, (META_BLOCK, META_BLOCK), 1)
    valid = col >= META_BLOCK - N_META
    fox_allowed = (col <= row) & (valid | (col == row))
    sb_allowed = (col < row) & valid
    suffix = (row >= col).astype(BF16)
    for hh in range(H):
        s = jnp.where(fox_allowed, _dot_nt(qkv_ref[hh], qkv_ref[H + hh]) + ncm_ref[hh], NEG_INF)
        p = jnp.exp2(s - jnp.max(s, axis=-1, keepdims=True))
        o = _dot(p.astype(BF16), qkv_ref[2 * H + hh]) / jnp.sum(p, axis=-1, keepdims=True)
        cols = slice(hh * HEAD_DIM, (hh + 1) * HEAD_DIM)
        o_ref[:, cols] = _head_norm(o, g_ref[:, cols]).astype(BF16)
    for hh in range(H):
        a, _ = _sb_weights(qkv_ref[3 * H + hh], qkv_ref[4 * H + hh], sb_allowed, suffix)
        cols = slice((H + hh) * HEAD_DIM, (H + hh + 1) * HEAD_DIM)
        o_ref[:, cols] = _head_norm(_dot(a, qkv_ref[5 * H + hh]), g_ref[:, cols]).astype(BF16)


def _meta_attention(qkv_lead, nc_lead, g_cat, *, n_heads):
    H = n_heads
    return pl.pallas_call(
        functools.partial(_meta_attn_kernel, n_heads=H),
        out_shape=jax.ShapeDtypeStruct((META_BLOCK, 2 * H * HEAD_DIM), BF16),
        name="meta_attention",
    )(qkv_lead, nc_lead.reshape(H, 1, -1), g_cat)


def _out_proj_kernel(h_ref, *rest, widths):
    *o_refs, w_ref, out_ref = rest
    acc = h_ref[...]
    start = 0
    for o_ref, width in zip(o_refs, widths):
        acc = acc + _dot(o_ref[...], w_ref[start:start + width, :])
        start += width
    out_ref[...] = acc


def _out_proj(h, parts, w, *, layer):
    rows, d = h.shape
    tm = _row_tile(rows, PROJ_ROWS)
    widths = tuple(p.shape[1] for p in parts)
    assert sum(widths) == w.shape[1]
    return pl.pallas_call(
        functools.partial(_out_proj_kernel, widths=widths),
        out_shape=jax.ShapeDtypeStruct((rows, d), F32),
        grid=(rows // tm,),
        in_specs=[pl.BlockSpec((tm, d), lambda i: (i, 0))]
        + [pl.BlockSpec((tm, width), lambda i: (i, 0)) for width in widths]
        + [pl.BlockSpec((None,) + w.shape[1:], lambda i: (layer, 0, 0))],
        out_specs=pl.BlockSpec((tm, d), lambda i: (i, 0)),
        compiler_params=pltpu.CompilerParams(
            dimension_semantics=("arbitrary",), vmem_limit_bytes=VMEM_LIMIT),
        name="out_proj",
    )(h, *parts, w)


def kernel(x, meta_tokens, ffn1_norm, ffn1_w_gate, ffn1_w_up, ffn1_w_down, mix_norm, w_in, b_forget, g_fox, g_sb, w_out, ffn2_norm, ffn2_w_gate, ffn2_w_up, ffn2_w_down, final_norm):
    batch, seq, d = x.shape
    depth = w_in.shape[0]
    n_heads = g_fox.shape[1] // HEAD_DIM
    hd = n_heads * HEAD_DIM
    f = ffn1_w_gate.shape[2]
    assert meta_tokens.shape[0] == N_META and w_in.shape[2] == 6 * hd + n_heads
    tf = 512 if f % 512 == 0 else LANES

    tok = x.reshape(batch * seq, d)
    lead = jnp.concatenate([jnp.zeros((META_BLOCK - N_META, d), x.dtype), meta_tokens.astype(x.dtype)], axis=0)

    row = lambda v: v.reshape(1, -1)
    ffn1_w = (ffn1_w_gate, ffn1_w_up, ffn1_w_down)
    w1 = [[_to_bf16(w, first_layer_only=True) for w in ffn1_w]]
    w_in_t = _to_bf16(jnp.swapaxes(w_in, 1, 2))
    for l in range(depth):
        last = l == depth - 1
        if l == 0:
            along = [(w, k) for k in range(1, depth) for w in ffn1_w] + [(w_out, None)]
            tok, cast, _ = _ffn(tok, row(ffn1_norm[l]), *w1[l], layer=0, tf=tf, cast_along=along)
            lead, _, _ = _ffn(lead, row(ffn1_norm[l]), *w1[l], layer=0, tf=tf)
            w1 += [[c[None] for c in cast[3 * (k - 1):3 * k]] for k in range(1, depth)]
            w_o = cast[-1].reshape(w_out.shape)
        else:
            tok, _, lead = _ffn(tok, row(ffn1_norm[l]), *w1[l], layer=0, tf=tf, lead=lead)

        qkv, fl = _proj(tok, row(mix_norm[l]), w_in_t, layer=l, n_heads=n_heads)
        qkv_lead, fl_lead = _proj(lead, row(mix_norm[l]), w_in_t, layer=l, n_heads=n_heads)
        nc, nc_lead = _scan(fl, fl_lead, b_forget[l], batch=batch, seq=seq)
        o_f, o_s, cast = _attention(qkv, qkv_lead, nc, nc_lead, row(g_fox[l]), row(g_sb[l]),
                                    batch=batch, seq=seq, n_heads=n_heads,
                                    cast_along=(ffn2_w_gate, ffn2_w_up, ffn2_w_down) if l == 0 else ())
        if l == 0:
            w2g, w2u, w2d = cast
        tok = _out_proj(tok, [o_f, o_s], w_o, layer=l)
        if last:
            tok, _, _ = _ffn(tok, row(ffn2_norm[l]), w2g, w2u, w2d, layer=l, tf=tf, final_gain=row(final_norm))
        else:
            o_m = _meta_attention(qkv_lead, nc_lead, row(jnp.concatenate([g_fox[l], g_sb[l]])), n_heads=n_heads)
            lead = _out_proj(lead, [o_m], w_o, layer=l)
            tok, _, lead = _ffn(tok, row(ffn2_norm[l]), w2g, w2u, w2d, layer=l, tf=tf, lead=lead)
    return tok.reshape(batch, seq, d)
```
